```python
import math
import jax, jax.numpy as jnp
from jax import lax
import numpy as np

D_MODEL = 4096
BATCH = 1
SEQ = 8192
DEPTH = 1

GRID_W = 64
CTX_LEN = 256
MLA_HEADS = 16
QK_NOPE = 128
QK_ROPE = 64
V_HEAD = 128
Q_LORA = 768
KV_LORA = 512
ROPE_THETA = 10000.0
Q_BLOCK = 128
SOFTMAX_SCALE = 1.0 / math.sqrt(QK_NOPE + QK_ROPE)
CONV_CH = 2048
CONV_GROUPS = 16
CONV_K = 3
PEER_HEADS = 8
N_KEYS = 128
N_EXPERTS = N_KEYS * N_KEYS
PEER_TOPK = 16
PEER_DK = 256
PEER_BLOCK = 32
MLA_WIDTH = MLA_HEADS * V_HEAD
MIX_WIDTH = MLA_WIDTH + CONV_CH
O_CQ = 0
O_CKV = Q_LORA
O_KR = Q_LORA + KV_LORA
O_CONV = O_KR + QK_ROPE
IN_COLS = O_CONV + 3 * CONV_CH
N_MOD = 6
EPS = 1e-6

kernel_name = "hybrid_mla_shortconv_peer_dit"


def rmsnorm(x, g):
    xf = x.astype(jnp.float32)
    y = xf * lax.rsqrt(jnp.mean(xf * xf, axis=-1, keepdims=True) + EPS)
    return (y * g.astype(jnp.float32)).astype(x.dtype)


def adaln(cond, w, b):
    m = jax.nn.silu(cond) @ w + b
    return m.reshape(*cond.shape[:-1], N_MOD, D_MODEL)


def modulate(h, shift, scale):
    return h * (1 + scale) + shift


def axial_tables(T, dtype):
    rows = T // GRID_W
    row = jnp.repeat(jnp.arange(rows), GRID_W).astype(jnp.float32)
    col = jnp.tile(jnp.arange(GRID_W), rows).astype(jnp.float32)
    half = QK_ROPE // 2
    freqs = ROPE_THETA ** (-jnp.arange(0, half, 2, dtype=jnp.float32) / half)
    ang_r = row[:, None] * freqs
    ang_c = col[:, None] * freqs
    return tuple(t.astype(dtype) for t in (jnp.cos(ang_r), jnp.sin(ang_r), jnp.cos(ang_c), jnp.sin(ang_c)))


def rotate(x, cos, sin):
    x1, x2 = jnp.split(x, 2, axis=-1)
    return jnp.concatenate([x1 * cos - x2 * sin, x2 * cos + x1 * sin], axis=-1)


def axial_rope(x, tabs):
    cr, sr, cc, sc = tabs
    xr, xc = jnp.split(x, 2, axis=-1)
    return jnp.concatenate([rotate(xr, cr, sr), rotate(xc, cc, sc)], axis=-1)


def mla_queries(cq, q_norm_g, w_uq):
    B, S, _ = cq.shape
    q = (rmsnorm(cq, q_norm_g) @ w_uq).reshape(B, S, MLA_HEADS, QK_NOPE + QK_ROPE)
    return q[..., :QK_NOPE], q[..., QK_NOPE:]


def mla_keys(ckv, kv_norm_g, w_ukv):
    B, S, _ = ckv.shape
    kv = (rmsnorm(ckv, kv_norm_g) @ w_ukv).reshape(B, S, MLA_HEADS, QK_NOPE + V_HEAD)
    return kv[..., :QK_NOPE], kv[..., QK_NOPE:]


def attend(q_nope, q_rope, k_nope, k_rope, v):
    B, S, H, _ = q_nope.shape
    nb = S // Q_BLOCK
    qn = q_nope.reshape(B, nb, Q_BLOCK, H, QK_NOPE).transpose(1, 0, 2, 3, 4)
    qr = q_rope.reshape(B, nb, Q_BLOCK, H, QK_ROPE).transpose(1, 0, 2, 3, 4)

    def block(args):
        qn_b, qr_b = args
        s = (jnp.einsum('bqhd,bkhd->bhqk', qn_b, k_nope)
             + jnp.einsum('bqhr,bkr->bhqk', qr_b, k_rope)).astype(jnp.float32) * SOFTMAX_SCALE
        p = jax.nn.softmax(s, axis=-1).astype(v.dtype)
        return jnp.einsum('bhqk,bkhd->bqhd', p, v)

    o = lax.map(block, (qn, qr))
    return o.transpose(1, 0, 2, 3, 4).reshape(B, S, H * V_HEAD)


def short_conv(pconv, w, b):
    xin, gb, gc = jnp.split(pconv, 3, axis=-1)
    u = jnp.pad(gc * xin, ((0, 0), (1, 1), (0, 0)))
    y = u[:, :-2] * w[0] + u[:, 1:-1] * w[1] + u[:, 2:] * w[2] + b
    return gb * y


def peer(h, w_q, keys, u, v):
    B, S, D = h.shape
    q = (h @ w_q).reshape(B, S, PEER_HEADS, 2, PEER_DK // 2)
    s = jnp.einsum('bshpd,hpkd->bshpk', q, keys)
    s1, i1 = lax.top_k(s[..., 0, :], PEER_TOPK)
    s2, i2 = lax.top_k(s[..., 1, :], PEER_TOPK)
    cand = (s1[..., :, None] + s2[..., None, :]).reshape(B, S, PEER_HEADS, PEER_TOPK * PEER_TOPK)
    top, flat = lax.top_k(cand, PEER_TOPK)
    e = (jnp.take_along_axis(i1, flat // PEER_TOPK, axis=-1) * N_KEYS
         + jnp.take_along_axis(i2, flat % PEER_TOPK, axis=-1))
    g = jax.nn.softmax(top.astype(jnp.float32), axis=-1).astype(h.dtype)
    nb = (B * S) // PEER_BLOCK
    hb = h.reshape(nb, PEER_BLOCK, D)
    eb = e.reshape(nb, PEER_BLOCK, PEER_HEADS * PEER_TOPK)
    gb = g.reshape(nb, PEER_BLOCK, PEER_HEADS * PEER_TOPK)

    def block(args):
        h_b, e_b, g_b = args
        act = jnp.einsum('td,tkd->tk', h_b, u[e_b])
        wgt = g_b * jax.nn.gelu(act, approximate=False)
        return jnp.einsum('tk,tkd->td', wgt, v[e_b])

    return lax.map(block, (hb, eb, gb)).reshape(B, S, D)


def setup_inputs(seed: int = 0) -> dict:
    key = jax.random.key(seed)
    ks = jax.random.split(key, 21)
    f32 = jnp.float32

    def nrm(k, shape, s):
        return jax.random.normal(k, shape, f32) * s

    D = D_MODEL
    return {
        "x": nrm(ks[0], (BATCH, SEQ, D), 1.0),
        "c": nrm(ks[1], (BATCH, D), 1.0),
        "ctx": nrm(ks[2], (BATCH, CTX_LEN, D), 1.0),
        "c_ctx": nrm(ks[3], (D,), 1.0),
        "w_ada": nrm(ks[4], (DEPTH, D, N_MOD * D), D ** -0.5),
        "b_ada": nrm(ks[5], (DEPTH, N_MOD * D), 0.01),
        "norm1_g": 1.0 + nrm(ks[6], (DEPTH, D), 0.02),
        "w_in": nrm(ks[7], (DEPTH, D, IN_COLS), D ** -0.5),
        "q_norm_g": 1.0 + nrm(ks[8], (DEPTH, Q_LORA), 0.02),
        "w_uq": nrm(ks[9], (DEPTH, Q_LORA, MLA_HEADS * (QK_NOPE + QK_ROPE)), Q_LORA ** -0.5),
        "kv_norm_g": 1.0 + nrm(ks[10], (DEPTH, KV_LORA), 0.02),
        "w_ukv": nrm(ks[11], (DEPTH, KV_LORA, MLA_HEADS * (QK_NOPE + V_HEAD)), KV_LORA ** -0.5),
        "conv_w": nrm(ks[12], (DEPTH, CONV_K, CONV_CH), CONV_K ** -0.5),
        "conv_b": nrm(ks[13], (DEPTH, CONV_CH), 0.01),
        "w_o": nrm(ks[14], (DEPTH, MIX_WIDTH, D), MIX_WIDTH ** -0.5),
        "norm2_g": 1.0 + nrm(ks[15], (DEPTH, D), 0.02),
        "peer_wq": nrm(ks[16], (DEPTH, D, PEER_HEADS * PEER_DK), D ** -0.5),
        "peer_keys": nrm(ks[17], (DEPTH, PEER_HEADS, 2, N_KEYS, PEER_DK // 2), (PEER_DK // 2) ** -0.5),
        "peer_u": nrm(ks[18], (DEPTH, N_EXPERTS, D), D ** -0.5),
        "peer_v": nrm(ks[19], (DEPTH, N_EXPERTS, D), PEER_HEADS ** -0.5),
        "final_norm_g": 1.0 + nrm(ks[20], (D,), 0.02),
    }


def reference(x, c, ctx, c_ctx, w_ada, b_ada, norm1_g, w_in, q_norm_g, w_uq, kv_norm_g,
              w_ukv, conv_w, conv_b, w_o, norm2_g, peer_wq, peer_keys, peer_u, peer_v,
              final_norm_g):
    T = x.shape[1]
    tabs_k = axial_tables(T, x.dtype)
    tabs_q = tuple(t[:, None, :] for t in tabs_k)
    xc = ctx
    for i in range(DEPTH):
        last = i == DEPTH - 1
        m_x = adaln(c, w_ada[i], b_ada[i])[:, None]
        m_c = adaln(c_ctx, w_ada[i], b_ada[i])

        hx = modulate(rmsnorm(x, norm1_g[i]), m_x[..., 0, :], m_x[..., 1, :])
        hc = modulate(rmsnorm(xc, norm1_g[i]), m_c[0], m_c[1])
        px = hx @ w_in[i]
        if last:
            pc = hc @ w_in[i][:, O_CKV:O_CONV]
            ckv_c, kr_c = pc[..., :KV_LORA], pc[..., KV_LORA:]
        else:
            pc = hc @ w_in[i]
            ckv_c, kr_c = pc[..., O_CKV:O_KR], pc[..., O_KR:O_CONV]

        kn_c, v_c = mla_keys(ckv_c, kv_norm_g[i], w_ukv[i])
        kn_x, v_x = mla_keys(px[..., O_CKV:O_KR], kv_norm_g[i], w_ukv[i])
        kr_x = axial_rope(px[..., O_KR:O_CONV], tabs_k)
        qn_x, qr_x = mla_queries(px[..., O_CQ:O_CKV], q_norm_g[i], w_uq[i])
        qr_x = axial_rope(qr_x, tabs_q)
        att_x = attend(qn_x, qr_x,
                       jnp.concatenate([kn_x, kn_c], axis=1),
                       jnp.concatenate([kr_x, kr_c], axis=1),
                       jnp.concatenate([v_x, v_c], axis=1))
        conv_x = short_conv(px[..., O_CONV:], conv_w[i], conv_b[i])
        x = x + m_x[..., 2, :] * (jnp.concatenate([att_x, conv_x], axis=-1) @ w_o[i])

        h2 = modulate(rmsnorm(x, norm2_g[i]), m_x[..., 3, :], m_x[..., 4, :])
        x = x + m_x[..., 5, :] * peer(h2, peer_wq[i], peer_keys[i], peer_u[i], peer_v[i])

        if not last:
            qn_c, qr_c = mla_queries(pc[..., O_CQ:O_CKV], q_norm_g[i], w_uq[i])
            att_c = attend(qn_c, qr_c, kn_c, kr_c, v_c)
            conv_c = short_conv(pc[..., O_CONV:], conv_w[i], conv_b[i])
            xc = xc + m_c[2] * (jnp.concatenate([att_c, conv_c], axis=-1) @ w_o[i])
            h2c = modulate(rmsnorm(xc, norm2_g[i]), m_c[3], m_c[4])
            xc = xc + m_c[5] * peer(h2c, peer_wq[i], peer_keys[i], peer_u[i], peer_v[i])
    return rmsnorm(x, final_norm_g)
```

```python
import functools
import math

import numpy as np
import jax
import jax.numpy as jnp
from jax import lax
from jax.experimental import pallas as pl
from jax.experimental.pallas import tpu as pltpu

F32 = jnp.float32
BF16 = jnp.bfloat16

D_MODEL = 4096
SEQ = 8192
GRID_W = 64
CTX_LEN = 256
MLA_HEADS = 16
QK_NOPE = 128
QK_ROPE = 64
V_HEAD = 128
Q_LORA = 768
KV_LORA = 512
ROPE_THETA = 10000.0
SOFTMAX_SCALE = 1.0 / math.sqrt(QK_NOPE + QK_ROPE)
CONV_CH = 2048
PEER_HEADS = 8
N_KEYS = 128
N_EXPERTS = N_KEYS * N_KEYS
PEER_TOPK = 16
PEER_DK = 256
N_MOD = 6
EPS = 1e-6
O_CKV = Q_LORA
O_KR = Q_LORA + KV_LORA
O_CONV = O_KR + QK_ROPE

LANE = 128
HEAD_PAD = 256
KV_LEN = SEQ + CTX_LEN
NEG = float(np.finfo(np.float32).min)
MIB = 1024 * 1024


def _cparams(sem, vmem_mib):
    return pltpu.CompilerParams(dimension_semantics=sem, vmem_limit_bytes=vmem_mib * MIB)


def _rms(x, g):
    ms = jnp.mean(x * x, axis=-1, keepdims=True)
    return x * lax.rsqrt(ms + EPS) * g


def _mods_kernel(cond_ref, w_ref, b_ref, o_ref):
    c = cond_ref[...]
    s = c * (1.0 / (1.0 + jnp.exp(-c)))
    o_ref[...] = jnp.dot(s.astype(BF16), w_ref[...].astype(BF16),
                         preferred_element_type=F32) + b_ref[...]


def _mods(cond8, w, b):
    n = w.shape[1]
    tn = 512
    return pl.pallas_call(
        _mods_kernel,
        grid=(n // tn,),
        in_specs=[pl.BlockSpec((8, D_MODEL), lambda j: (0, 0)),
                  pl.BlockSpec((D_MODEL, tn), lambda j: (0, j)),
                  pl.BlockSpec((1, tn), lambda j: (0, j))],
        out_specs=pl.BlockSpec((8, tn), lambda j: (0, j)),
        out_shape=jax.ShapeDtypeStruct((8, n), F32),
        compiler_params=_cparams(("arbitrary",), 48),
        name="mods",
    )(cond8, w, b.reshape(1, n))


def _norm_proj_kernel(*refs, rope_last, transpose_h, row_chunk):
    if rope_last:
        x_ref, g_ref, sh_ref, sc_ref, w_ref, cos_ref, sin_ref, o_ref, h_ref, h_scr = refs
    else:
        x_ref, g_ref, sh_ref, sc_ref, w_ref, o_ref, h_ref, h_scr = refs
    tm = x_ref.shape[0]

    @pl.when(pl.program_id(1) == 0)
    def _():
        for r in range(tm // row_chunk):
            rows = slice(r * row_chunk, (r + 1) * row_chunk)
            y = _rms(x_ref[rows, :], g_ref[...])
            h = y * (1.0 + sc_ref[...]) + sh_ref[...]
            hb = h.astype(BF16)
            h_scr[rows, :] = hb
            if transpose_h:
                h_ref[:, rows] = h.T.astype(BF16)
            else:
                h_ref[rows, :] = hb

    acc = jnp.dot(h_scr[...], w_ref[...], preferred_element_type=F32)
    if rope_last:
        n = acc.shape[1]
        o_ref[:, : n - LANE] = acc[:, : n - LANE].astype(o_ref.dtype)
        xr = acc[:, n - LANE:]
        o_ref[:, n - LANE:] = (xr * cos_ref[...]
                               + pltpu.roll(xr, LANE // 2, axis=1) * sin_ref[...]).astype(o_ref.dtype)
    else:
        o_ref[...] = acc.astype(o_ref.dtype)


def _norm_proj(x, g, shift, scale, w, *, tm, tn, out_dtype, cos=None, sin=None,
               transpose_h=False, vmem_mib=56):
    m, k = x.shape
    n = w.shape[1]
    rope_last = cos is not None
    row = lambda a: a.reshape(1, k)
    in_specs = [pl.BlockSpec((tm, k), lambda i, j: (i, 0)),
                pl.BlockSpec((1, k), lambda i, j: (0, 0)),
                pl.BlockSpec((1, k), lambda i, j: (0, 0)),
                pl.BlockSpec((1, k), lambda i, j: (0, 0)),
                pl.BlockSpec((k, tn), lambda i, j: (0, j))]
    args = [x, row(g), row(shift), row(scale), w]
    if rope_last:
        assert tn == n
        in_specs += [pl.BlockSpec((tm, LANE), lambda i, j: (i, 0)),
                     pl.BlockSpec((tm, LANE), lambda i, j: (i, 0))]
        args += [cos, sin]
    if transpose_h:
        h_shape, h_spec = (k, m), pl.BlockSpec((k, tm), lambda i, j: (0, i))
    else:
        h_shape, h_spec = (m, k), pl.BlockSpec((tm, k), lambda i, j: (i, 0))
    return pl.pallas_call(
        functools.partial(_norm_proj_kernel, rope_last=rope_last, transpose_h=transpose_h,
                          row_chunk=min(tm, LANE)),
        grid=(m // tm, n // tn),
        in_specs=in_specs,
        out_specs=[pl.BlockSpec((tm, tn), lambda i, j: (i, j)), h_spec],
        out_shape=[jax.ShapeDtypeStruct((m, n), out_dtype),
                   jax.ShapeDtypeStruct(h_shape, BF16)],
        scratch_shapes=[pltpu.VMEM((tm, k), BF16)],
        compiler_params=_cparams(("arbitrary", "arbitrary"), vmem_mib),
        name="norm_proj",
    )(*args)


def _matmul_kernel(a_ref, w_ref, o_ref):
    o_ref[...] = jnp.dot(a_ref[...], w_ref[...], preferred_element_type=F32).astype(o_ref.dtype)


def _matmul(a, w, *, tm, tn, out_dtype, vmem_mib=48):
    m, k = a.shape
    n = w.shape[1]
    return pl.pallas_call(
        _matmul_kernel,
        grid=(m // tm, n // tn),
        in_specs=[pl.BlockSpec((tm, k), lambda i, j: (i, 0)),
                  pl.BlockSpec((k, tn), lambda i, j: (0, j))],
        out_specs=pl.BlockSpec((tm, tn), lambda i, j: (i, j)),
        out_shape=jax.ShapeDtypeStruct((m, n), out_dtype),
        compiler_params=_cparams(("arbitrary", "arbitrary"), vmem_mib),
        name="matmul",
    )(a, w)


def _conv_kernel(xin_ref, gb_ref, gc_ref, w_ref, b_ref, o_ref):
    t = xin_ref.shape[0]
    u = gc_ref[...] * xin_ref[...]
    row = lax.broadcasted_iota(jnp.int32, u.shape, 0)
    prev = jnp.where(row == 0, 0.0, pltpu.roll(u, 1, axis=0))
    nxt = jnp.where(row == t - 1, 0.0, pltpu.roll(u, t - 1, axis=0))
    y = prev * w_ref[0:1, :] + u * w_ref[1:2, :] + nxt * w_ref[2:3, :] + b_ref[...]
    o_ref[...] = (gb_ref[...] * y).astype(o_ref.dtype)


def _short_conv(pconv, w, b):
    t = pconv.shape[0]
    nb = CONV_CH // LANE
    return pl.pallas_call(
        _conv_kernel,
        grid=(nb,),
        in_specs=[pl.BlockSpec((t, LANE), lambda j: (0, j)),
                  pl.BlockSpec((t, LANE), lambda j: (0, nb + j)),
                  pl.BlockSpec((t, LANE), lambda j: (0, 2 * nb + j)),
                  pl.BlockSpec((3, LANE), lambda j: (0, j)),
                  pl.BlockSpec((1, LANE), lambda j: (0, j))],
        out_specs=pl.BlockSpec((t, LANE), lambda j: (0, j)),
        out_shape=jax.ShapeDtypeStruct((t, CONV_CH), BF16),
        compiler_params=_cparams(("arbitrary",), 56),
        name="short_conv",
    )(pconv, pconv, pconv, w, b.reshape(1, CONV_CH))


def _q_proj_kernel(cq_ref, g_ref, w_ref, cos_ref, sin_ref, q_ref):
    y = _rms(cq_ref[...], g_ref[...]).astype(BF16)
    acc = jnp.dot(y, w_ref[...], preferred_element_type=F32)
    cos = cos_ref[...]
    sin = sin_ref[...]
    for h in range(MLA_HEADS):
        lo = h * HEAD_PAD
        q_ref[:, lo:lo + LANE] = (acc[:, lo:lo + LANE] * SOFTMAX_SCALE).astype(BF16)
        xr = acc[:, lo + LANE:lo + HEAD_PAD]
        rope = xr * cos + pltpu.roll(xr, LANE // 2, axis=1) * sin
        q_ref[:, lo + LANE:lo + HEAD_PAD] = (rope * SOFTMAX_SCALE).astype(BF16)


def _q_proj(pa, g, w, cos, sin, *, tm):
    m = pa.shape[0]
    n = w.shape[1]
    return pl.pallas_call(
        _q_proj_kernel,
        grid=(m // tm,),
        in_specs=[pl.BlockSpec((tm, Q_LORA), lambda i: (i, 0)),
                  pl.BlockSpec((1, Q_LORA), lambda i: (0, 0)),
                  pl.BlockSpec((Q_LORA, n), lambda i: (0, 0)),
                  pl.BlockSpec((tm, LANE), lambda i: (i, 0)),
                  pl.BlockSpec((tm, LANE), lambda i: (i, 0))],
        out_specs=pl.BlockSpec((tm, n), lambda i: (i, 0)),
        out_shape=jax.ShapeDtypeStruct((m, n), BF16),
        compiler_params=_cparams(("arbitrary",), 48),
        name="q_proj",
    )(pa, g.reshape(1, Q_LORA), w, cos, sin)


def _kv_proj_kernel(ckv_ref, g_ref, kr_ref, wk_ref, wv_ref, k_ref, v_ref):
    y = _rms(ckv_ref[...], g_ref[...]).astype(BF16)
    kn = jnp.dot(y, wk_ref[...], preferred_element_type=F32)
    v_ref[...] = jnp.dot(y, wv_ref[...], preferred_element_type=F32).astype(BF16)
    kr = kr_ref[...].astype(BF16)
    for h in range(MLA_HEADS):
        lo = h * HEAD_PAD
        k_ref[:, lo:lo + LANE] = kn[:, h * QK_NOPE:(h + 1) * QK_NOPE].astype(BF16)
        k_ref[:, lo + LANE:lo + HEAD_PAD] = kr


def _kv_proj(ckv, g, krot, wk, wv, *, tm):
    m = ckv.shape[0]
    return pl.pallas_call(
        _kv_proj_kernel,
        grid=(m // tm,),
        in_specs=[pl.BlockSpec((tm, KV_LORA), lambda i: (i, 0)),
                  pl.BlockSpec((1, KV_LORA), lambda i: (0, 0)),
                  pl.BlockSpec((tm, LANE), lambda i: (i, 0)),
                  pl.BlockSpec(wk.shape, lambda i: (0, 0)),
                  pl.BlockSpec(wv.shape, lambda i: (0, 0))],
        out_specs=[pl.BlockSpec((tm, MLA_HEADS * HEAD_PAD), lambda i: (i, 0)),
                   pl.BlockSpec((tm, MLA_HEADS * V_HEAD), lambda i: (i, 0))],
        out_shape=[jax.ShapeDtypeStruct((m, MLA_HEADS * HEAD_PAD), BF16),
                   jax.ShapeDtypeStruct((m, MLA_HEADS * V_HEAD), BF16)],
        compiler_params=_cparams(("arbitrary",), 48),
        name="kv_proj",
    )(ckv, g.reshape(1, KV_LORA), krot, wk, wv)


def _attn_kernel(q_ref, k_ref, v_ref, o_ref, m_scr, l_scr, acc_scr, *, chunk):
    q = q_ref[...]
    m_scr[...] = jnp.full(m_scr.shape, NEG, F32)
    l_scr[...] = jnp.zeros(l_scr.shape, F32)
    acc_scr[...] = jnp.zeros(acc_scr.shape, F32)

    def body(c, carry):
        rows = pl.ds(pl.multiple_of(c * chunk, chunk), chunk)
        s = lax.dot_general(q, k_ref[rows, :], (((1,), (1,)), ((), ())),
                            preferred_element_type=F32)
        m_old = m_scr[...]
        m_new = jnp.maximum(m_old, jnp.max(s, axis=-1, keepdims=True))
        p = jnp.exp(s - m_new)
        alpha = jnp.exp(m_old - m_new)
        l_scr[...] = alpha * l_scr[...] + jnp.sum(p, axis=-1, keepdims=True)
        acc_scr[...] = alpha * acc_scr[...] + jnp.dot(
            p.astype(BF16), v_ref[rows, :], preferred_element_type=F32)
        m_scr[...] = m_new
        return carry

    lax.fori_loop(0, k_ref.shape[0] // chunk, body, 0)
    o_ref[...] = (acc_scr[...] / l_scr[...]).astype(o_ref.dtype)


def _attention(q, k, v, *, tq, chunk):
    s = q.shape[0]
    kv = k.shape[0]
    return pl.pallas_call(
        functools.partial(_attn_kernel, chunk=chunk),
        grid=(MLA_HEADS, s // tq),
        in_specs=[pl.BlockSpec((tq, HEAD_PAD), lambda h, i: (i, h)),
                  pl.BlockSpec((kv, HEAD_PAD), lambda h, i: (0, h)),
                  pl.BlockSpec((kv, V_HEAD), lambda h, i: (0, h))],
        out_specs=pl.BlockSpec((tq, V_HEAD), lambda h, i: (i, h)),
        out_shape=jax.ShapeDtypeStruct((s, MLA_HEADS * V_HEAD), BF16),
        scratch_shapes=[pltpu.VMEM((tq, 1), F32), pltpu.VMEM((tq, 1), F32),
                        pltpu.VMEM((tq, V_HEAD), F32)],
        compiler_params=_cparams(("arbitrary", "arbitrary"), 48),
        name="attention",
    )(q, k, v)


def _out_proj_kernel(att_ref, conv_ref, wa_ref, wc_ref, x_ref, gate_ref, o_ref):
    acc = jnp.dot(att_ref[...], wa_ref[...], preferred_element_type=F32)
    acc = acc + jnp.dot(conv_ref[...], wc_ref[...], preferred_element_type=F32)
    o_ref[...] = x_ref[...] + gate_ref[...] * acc


def _out_proj(att, conv, wa, wc, x, gate, *, tm, tn):
    m, ka = att.shape
    kc = conv.shape[1]
    n = wa.shape[1]
    return pl.pallas_call(
        _out_proj_kernel,
        grid=(m // tm, n // tn),
        in_specs=[pl.BlockSpec((tm, ka), lambda i, j: (i, 0)),
                  pl.BlockSpec((tm, kc), lambda i, j: (i, 0)),
                  pl.BlockSpec((ka, tn), lambda i, j: (0, j)),
                  pl.BlockSpec((kc, tn), lambda i, j: (0, j)),
                  pl.BlockSpec((tm, tn), lambda i, j: (i, j)),
                  pl.BlockSpec((1, tn), lambda i, j: (0, j))],
        out_specs=pl.BlockSpec((tm, tn), lambda i, j: (i, j)),
        out_shape=jax.ShapeDtypeStruct((m, n), F32),
        compiler_params=_cparams(("arbitrary", "arbitrary"), 48),
        name="out_proj",
    )(att, conv, wa, wc, x, gate.reshape(1, n))


def _top16(pieces, lanes):
    rid = lax.broadcasted_iota(jnp.int32, (PEER_TOPK, lanes), 0)
    tops = jnp.zeros((PEER_TOPK, lanes), F32)
    for k in range(PEER_TOPK):
        m = pieces[0].max(axis=0, keepdims=True)
        for p in pieces[1:]:
            m = jnp.maximum(m, p.max(axis=0, keepdims=True))
        tops = jnp.where(rid == k, m, tops)
        if k + 1 < PEER_TOPK:
            pieces = [jnp.where(p == m, NEG, p) for p in pieces]
    return tops


def _peer_gate_kernel(q_ref, keys_ref, s1_ref, e1_ref, s2_ref, e2_ref, thr_ref):
    lanes = q_ref.shape[0]
    nt = (((1,), (1,)), ((), ()))
    for h in range(PEER_HEADS):
        q1 = q_ref[:, (2 * h) * LANE:(2 * h + 1) * LANE]
        q2 = q_ref[:, (2 * h + 1) * LANE:(2 * h + 2) * LANE]
        s1 = lax.dot_general(keys_ref[2 * h], q1, nt, preferred_element_type=F32)
        s2 = lax.dot_general(keys_ref[2 * h + 1], q2, nt, preferred_element_type=F32)
        t1 = _top16([s1], lanes)
        t2 = _top16([s2], lanes)
        cands = [t1[0:1] + t2]
        cands += [t1[i:i + 1] + t2[0:8] for i in range(1, 8)]
        cands += [t1[8:16] + t2[0:1]]
        top = _top16(cands, lanes)
        z = jnp.sum(jnp.exp(top - top[0:1]), axis=0, keepdims=True)
        s1_ref[h] = s1
        s2_ref[h] = s2
        e1_ref[h] = jnp.exp(s1 - t1[0:1]) / z
        e2_ref[h] = jnp.exp(s2 - t2[0:1])
        thr_ref[h] = top[PEER_TOPK - 1:PEER_TOPK]


def _peer_gates(q, keys, *, tt):
    t = q.shape[0]
    tab = jax.ShapeDtypeStruct((PEER_HEADS, N_KEYS, t), F32)
    tab_spec = pl.BlockSpec((PEER_HEADS, N_KEYS, tt), lambda i: (0, 0, i))
    return pl.pallas_call(
        _peer_gate_kernel,
        grid=(t // tt,),
        in_specs=[pl.BlockSpec((tt, PEER_HEADS * PEER_DK), lambda i: (i, 0)),
                  pl.BlockSpec(keys.shape, lambda i: (0, 0, 0))],
        out_specs=[tab_spec, tab_spec, tab_spec, tab_spec,
                   pl.BlockSpec((PEER_HEADS, 1, tt), lambda i: (0, 0, i))],
        out_shape=[tab, tab, tab, tab, jax.ShapeDtypeStruct((PEER_HEADS, 1, t), F32)],
        compiler_params=_cparams(("arbitrary",), 48),
        name="peer_gates",
    )(q, keys)


def _peer_kernel(ht_ref, u_ref, vt_ref, s1_ref, e1_ref, s2_ref, e2_ref, thr_ref, o_ref, w_scr):
    j = pl.program_id(1)
    te, tm = w_scr.shape

    @pl.when(j == 0)
    def _():
        o_ref[...] = jnp.zeros(o_ref.shape, F32)

    act = jnp.dot(u_ref[...], ht_ref[...], preferred_element_type=F32)
    gel = 0.5 * act * (1.0 + lax.erf(act * (1.0 / math.sqrt(2.0))))
    for a in range(te // N_KEYS):
        for c in range(tm // LANE):
            cols = slice(c * LANE, (c + 1) * LANE)
            g = jnp.zeros((N_KEYS, LANE), F32)
            for h in range(PEER_HEADS):
                s1row = s1_ref[h, 0, a:a + 1, cols]
                e1row = e1_ref[h, 0, a:a + 1, cols]
                sel = jnp.where(s1row + s2_ref[h, :, cols] >= thr_ref[h, :, cols],
                                e2_ref[h, :, cols], 0.0)
                g = g + e1row * sel
            w_scr[a * N_KEYS:(a + 1) * N_KEYS, cols] = (
                g * gel[a * N_KEYS:(a + 1) * N_KEYS, cols]).astype(BF16)
    o_ref[...] += jnp.dot(vt_ref[...], w_scr[...], preferred_element_type=F32)


def _peer_dense(ht, u, vt, tabs, *, tm, te):
    d, t = ht.shape
    ne = u.shape[0]
    s1, e1, s2, e2, thr = tabs
    na = te // N_KEYS
    s1 = s1.reshape(PEER_HEADS, N_KEYS // na, na, t)
    e1 = e1.reshape(PEER_HEADS, N_KEYS // na, na, t)
    row_spec = pl.BlockSpec((PEER_HEADS, 1, na, tm), lambda i, j: (0, j, 0, i))
    tab_spec = pl.BlockSpec((PEER_HEADS, N_KEYS, tm), lambda i, j: (0, 0, i))
    return pl.pallas_call(
        _peer_kernel,
        grid=(t // tm, ne // te),
        in_specs=[pl.BlockSpec((d, tm), lambda i, j: (0, i)),
                  pl.BlockSpec((te, d), lambda i, j: (j, 0)),
                  pl.BlockSpec((d, te), lambda i, j: (0, j)),
                  row_spec, row_spec, tab_spec, tab_spec,
                  pl.BlockSpec((PEER_HEADS, 1, tm), lambda i, j: (0, 0, i))],
        out_specs=pl.BlockSpec((d, tm), lambda i, j: (0, i)),
        out_shape=jax.ShapeDtypeStruct((d, t), F32),
        scratch_shapes=[pltpu.VMEM((te, tm), BF16)],
        compiler_params=_cparams(("arbitrary", "arbitrary"), 56),
        name="peer_dense",
    )(ht, u, vt, s1, e1, s2, e2, thr)


def _final_kernel(pt_ref, x_ref, gate_ref, g_ref, o_ref):
    x2 = x_ref[...] + gate_ref[...] * pt_ref[...].T
    o_ref[...] = _rms(x2, g_ref[...])


def _final(pt, x1, gate, g, *, tf):
    d, t = pt.shape
    return pl.pallas_call(
        _final_kernel,
        grid=(t // tf,),
        in_specs=[pl.BlockSpec((d, tf), lambda i: (0, i)),
                  pl.BlockSpec((tf, d), lambda i: (i, 0)),
                  pl.BlockSpec((1, d), lambda i: (0, 0)),
                  pl.BlockSpec((1, d), lambda i: (0, 0))],
        out_specs=pl.BlockSpec((tf, d), lambda i: (i, 0)),
        out_shape=jax.ShapeDtypeStruct((t, d), F32),
        compiler_params=_cparams(("arbitrary",), 48),
        name="final",
    )(pt, x1, gate.reshape(1, d), g.reshape(1, d))


def _rope_perm():
    j = np.arange(QK_ROPE)
    return np.where((j % 32) < 16, j + 16, j - 16)


def _rope_tables(t):
    rows = t // GRID_W
    row = jnp.repeat(jnp.arange(rows), GRID_W).astype(F32)
    col = jnp.tile(jnp.arange(GRID_W), rows).astype(F32)
    half = QK_ROPE // 2
    freqs = ROPE_THETA ** (-jnp.arange(0, half, 2, dtype=F32) / half)
    ar = row[:, None] * freqs
    ac = col[:, None] * freqs
    zeros = jnp.zeros((t, LANE - QK_ROPE), F32)
    cos = jnp.concatenate([jnp.cos(ar), jnp.cos(ar), jnp.cos(ac), jnp.cos(ac), zeros], axis=1)
    sin = jnp.concatenate([-jnp.sin(ar), jnp.sin(ar), -jnp.sin(ac), jnp.sin(ac), zeros], axis=1)
    return cos, sin


def kernel(x, c, ctx, c_ctx, w_ada, b_ada, norm1_g, w_in, q_norm_g, w_uq, kv_norm_g, w_ukv,
           conv_w, conv_b, w_o, norm2_g, peer_wq, peer_keys, peer_u, peer_v, final_norm_g):
    assert x.shape == (1, SEQ, D_MODEL) and ctx.shape == (1, CTX_LEN, D_MODEL)
    assert w_ada.shape[0] == 1
    x2 = x[0]
    ctx2 = ctx[0]
    perm = _rope_perm()

    w_in0 = w_in[0]
    w_kr_perm = w_in0[:, O_KR:O_CONV][:, perm]
    w_a_x = jnp.concatenate([w_in0[:, :O_CONV], w_kr_perm], axis=1).astype(BF16)
    w_a_c = jnp.concatenate([w_in0[:, O_CKV:O_CONV], w_kr_perm], axis=1).astype(BF16)
    w_conv = w_in0[:, O_CONV:].astype(BF16)
    wq = w_uq[0].reshape(Q_LORA, MLA_HEADS, QK_NOPE + QK_ROPE)
    wq = jnp.concatenate([wq, wq[:, :, QK_NOPE:][:, :, perm]], axis=2)
    wq = wq.reshape(Q_LORA, MLA_HEADS * HEAD_PAD).astype(BF16)
    wkv = w_ukv[0].reshape(KV_LORA, MLA_HEADS, QK_NOPE + V_HEAD)
    wk = wkv[:, :, :QK_NOPE].reshape(KV_LORA, MLA_HEADS * QK_NOPE).astype(BF16)
    wv = wkv[:, :, QK_NOPE:].reshape(KV_LORA, MLA_HEADS * V_HEAD).astype(BF16)
    wo_att = w_o[0, :MLA_HEADS * V_HEAD].astype(BF16)
    wo_conv = w_o[0, MLA_HEADS * V_HEAD:].astype(BF16)
    w_pq = peer_wq[0].astype(BF16)
    keys = peer_keys[0].reshape(PEER_HEADS * 2, N_KEYS, PEER_DK // 2).astype(BF16)
    u_bf = peer_u[0].astype(BF16)
    vt_bf = peer_v[0].astype(BF16).T

    cos_x, sin_x = _rope_tables(SEQ)
    cos_c = jnp.concatenate([jnp.ones((CTX_LEN, QK_ROPE), F32),
                             jnp.zeros((CTX_LEN, LANE - QK_ROPE), F32)], axis=1)
    sin_c = jnp.zeros((CTX_LEN, LANE), F32)

    cond8 = jnp.concatenate([c, c_ctx[None, :], jnp.zeros((6, D_MODEL), F32)], axis=0)
    mods = _mods(cond8, w_ada[0], b_ada[0])
    m_x = mods[0].reshape(N_MOD, D_MODEL)
    m_c = mods[1].reshape(N_MOD, D_MODEL)

    pa_x, hx = _norm_proj(x2, norm1_g[0], m_x[0], m_x[1], w_a_x, tm=256, tn=w_a_x.shape[1],
                          out_dtype=F32, cos=cos_x, sin=sin_x)
    pa_c, _ = _norm_proj(ctx2, norm1_g[0], m_c[0], m_c[1], w_a_c, tm=CTX_LEN, tn=w_a_c.shape[1],
                         out_dtype=F32, cos=cos_c, sin=sin_c)
    pconv = _matmul(hx, w_conv, tm=1024, tn=1024, out_dtype=F32)
    conv = _short_conv(pconv, conv_w[0], conv_b[0])

    q = _q_proj(pa_x, q_norm_g[0], wq, cos_x, sin_x, tm=256)
    ckv = jnp.concatenate([pa_x[:, O_CKV:O_KR], pa_c[:, :KV_LORA]], axis=0)
    krot = jnp.concatenate([pa_x[:, O_KR:], pa_c[:, KV_LORA:]], axis=0)
    k, v = _kv_proj(ckv, kv_norm_g[0], krot, wk, wv, tm=256)
    att = _attention(q, k, v, tq=512, chunk=768)

    x1 = _out_proj(att, conv, wo_att, wo_conv, x2, m_x[2], tm=1024, tn=512)

    qp, h2t = _norm_proj(x1, norm2_g[0], m_x[3], m_x[4], w_pq, tm=512, tn=1024,
                         out_dtype=BF16, transpose_h=True)
    tabs = _peer_gates(qp, keys, tt=512)
    pt = _peer_dense(h2t, u_bf, vt_bf, tabs, tm=512, te=256)
    out = _final(pt, x1, m_x[5], final_norm_g, tf=256)
    return out[None]
```

```python
import functools
import math

import numpy as np
import jax
import jax.numpy as jnp
from jax import lax
from jax.experimental import pallas as pl
from jax.experimental.pallas import tpu as pltpu

F32 = jnp.float32
BF16 = jnp.bfloat16

D_MODEL = 4096
SEQ = 8192
GRID_W = 64
CTX_LEN = 256
MLA_HEADS = 16
QK_NOPE = 128
QK_ROPE = 64
V_HEAD = 128
Q_LORA = 768
KV_LORA = 512
ROPE_THETA = 10000.0
SOFTMAX_SCALE = 1.0 / math.sqrt(QK_NOPE + QK_ROPE)
CONV_CH = 2048
PEER_HEADS = 8
N_KEYS = 128
N_EXPERTS = N_KEYS * N_KEYS
PEER_TOPK = 16
PEER_DK = 256
N_MOD = 6
EPS = 1e-6
O_CKV = Q_LORA
O_KR = Q_LORA + KV_LORA
O_CONV = O_KR + QK_ROPE

LANE = 128
HEAD_PAD = 256
KV_LEN = SEQ + CTX_LEN
KV_CHUNK = 768
Q_SCALE = SOFTMAX_SCALE * math.log2(math.e)
NEG = float(np.finfo(np.float32).min)
MIB = 1024 * 1024


def _cparams(sem, vmem_mib):
    return pltpu.CompilerParams(dimension_semantics=sem, vmem_limit_bytes=vmem_mib * MIB)


def _rms(x, g):
    ms = jnp.mean(x * x, axis=-1, keepdims=True)
    return x * lax.rsqrt(ms + EPS) * g


def _mods_kernel(cond_ref, w_ref, b_ref, o_ref):
    c = cond_ref[...]
    s = c * (1.0 / (1.0 + jnp.exp(-c)))
    o_ref[...] = jnp.dot(s.astype(BF16), w_ref[...].astype(BF16),
                         preferred_element_type=F32) + b_ref[...]


def _mods(cond8, w, b):
    n = w.shape[1]
    tn = 512
    return pl.pallas_call(
        _mods_kernel,
        grid=(n // tn,),
        in_specs=[pl.BlockSpec((8, D_MODEL), lambda j: (0, 0)),
                  pl.BlockSpec((D_MODEL, tn), lambda j: (0, j)),
                  pl.BlockSpec((1, tn), lambda j: (0, j))],
        out_specs=pl.BlockSpec((8, tn), lambda j: (0, j)),
        out_shape=jax.ShapeDtypeStruct((8, n), F32),
        compiler_params=_cparams(("arbitrary",), 48),
        name="mods",
    )(cond8, w, b.reshape(1, n))


def _norm_proj_kernel(*refs, rope_last, transpose_h, row_chunk):
    if rope_last:
        x_ref, g_ref, sh_ref, sc_ref, w_ref, cos_ref, sin_ref, o_ref, h_ref, h_scr = refs
    else:
        x_ref, g_ref, sh_ref, sc_ref, w_ref, o_ref, h_ref, h_scr = refs
    tm = x_ref.shape[0]

    @pl.when(pl.program_id(1) == 0)
    def _():
        for r in range(tm // row_chunk):
            rows = slice(r * row_chunk, (r + 1) * row_chunk)
            y = _rms(x_ref[rows, :], g_ref[...])
            h = y * (1.0 + sc_ref[...]) + sh_ref[...]
            hb = h.astype(BF16)
            h_scr[rows, :] = hb
            if transpose_h:
                h_ref[:, rows] = h.T.astype(BF16)
            else:
                h_ref[rows, :] = hb

    acc = jnp.dot(h_scr[...], w_ref[...], preferred_element_type=F32)
    if rope_last:
        n = acc.shape[1]
        o_ref[:, : n - LANE] = acc[:, : n - LANE].astype(o_ref.dtype)
        xr = acc[:, n - LANE:]
        o_ref[:, n - LANE:] = (xr * cos_ref[...]
                               + pltpu.roll(xr, LANE // 2, axis=1) * sin_ref[...]).astype(o_ref.dtype)
    else:
        o_ref[...] = acc.astype(o_ref.dtype)


def _norm_proj(x, g, shift, scale, w, *, tm, tn, out_dtype, cos=None, sin=None,
               transpose_h=False, vmem_mib=56):
    m, k = x.shape
    n = w.shape[1]
    rope_last = cos is not None
    row = lambda a: a.reshape(1, k)
    in_specs = [pl.BlockSpec((tm, k), lambda i, j: (i, 0)),
                pl.BlockSpec((1, k), lambda i, j: (0, 0)),
                pl.BlockSpec((1, k), lambda i, j: (0, 0)),
                pl.BlockSpec((1, k), lambda i, j: (0, 0)),
                pl.BlockSpec((k, tn), lambda i, j: (0, j))]
    args = [x, row(g), row(shift), row(scale), w]
    if rope_last:
        assert tn == n
        in_specs += [pl.BlockSpec((tm, LANE), lambda i, j: (i, 0)),
                     pl.BlockSpec((tm, LANE), lambda i, j: (i, 0))]
        args += [cos, sin]
    if transpose_h:
        h_shape, h_spec = (k, m), pl.BlockSpec((k, tm), lambda i, j: (0, i))
    else:
        h_shape, h_spec = (m, k), pl.BlockSpec((tm, k), lambda i, j: (i, 0))
    return pl.pallas_call(
        functools.partial(_norm_proj_kernel, rope_last=rope_last, transpose_h=transpose_h,
                          row_chunk=min(tm, LANE)),
        grid=(m // tm, n // tn),
        in_specs=in_specs,
        out_specs=[pl.BlockSpec((tm, tn), lambda i, j: (i, j)), h_spec],
        out_shape=[jax.ShapeDtypeStruct((m, n), out_dtype),
                   jax.ShapeDtypeStruct(h_shape, BF16)],
        scratch_shapes=[pltpu.VMEM((tm, k), BF16)],
        compiler_params=_cparams(("arbitrary", "arbitrary"), vmem_mib),
        name="norm_proj",
    )(*args)


def _matmul_kernel(a_ref, w_ref, o_ref):
    o_ref[...] = jnp.dot(a_ref[...], w_ref[...], preferred_element_type=F32).astype(o_ref.dtype)


def _matmul(a, w, *, tm, tn, out_dtype, vmem_mib=48):
    m, k = a.shape
    n = w.shape[1]
    return pl.pallas_call(
        _matmul_kernel,
        grid=(m // tm, n // tn),
        in_specs=[pl.BlockSpec((tm, k), lambda i, j: (i, 0)),
                  pl.BlockSpec((k, tn), lambda i, j: (0, j))],
        out_specs=pl.BlockSpec((tm, tn), lambda i, j: (i, j)),
        out_shape=jax.ShapeDtypeStruct((m, n), out_dtype),
        compiler_params=_cparams(("arbitrary", "arbitrary"), vmem_mib),
        name="matmul",
    )(a, w)


def _conv_kernel(xin_ref, gb_ref, gc_ref, w_ref, b_ref, o_ref):
    t = xin_ref.shape[0]
    u = gc_ref[...] * xin_ref[...]
    row = lax.broadcasted_iota(jnp.int32, u.shape, 0)
    prev = jnp.where(row == 0, 0.0, pltpu.roll(u, 1, axis=0))
    nxt = jnp.where(row == t - 1, 0.0, pltpu.roll(u, t - 1, axis=0))
    y = prev * w_ref[0:1, :] + u * w_ref[1:2, :] + nxt * w_ref[2:3, :] + b_ref[...]
    o_ref[...] = (gb_ref[...] * y).astype(o_ref.dtype)


def _short_conv(pconv, w, b):
    t = pconv.shape[0]
    nb = CONV_CH // LANE
    return pl.pallas_call(
        _conv_kernel,
        grid=(nb,),
        in_specs=[pl.BlockSpec((t, LANE), lambda j: (0, j)),
                  pl.BlockSpec((t, LANE), lambda j: (0, nb + j)),
                  pl.BlockSpec((t, LANE), lambda j: (0, 2 * nb + j)),
                  pl.BlockSpec((3, LANE), lambda j: (0, j)),
                  pl.BlockSpec((1, LANE), lambda j: (0, j))],
        out_specs=pl.BlockSpec((t, LANE), lambda j: (0, j)),
        out_shape=jax.ShapeDtypeStruct((t, CONV_CH), BF16),
        compiler_params=_cparams(("arbitrary",), 56),
        name="short_conv",
    )(pconv, pconv, pconv, w, b.reshape(1, CONV_CH))


def _q_proj_kernel(cq_ref, g_ref, w_ref, cos_ref, sin_ref, q_ref):
    y = _rms(cq_ref[...], g_ref[...]).astype(BF16)
    acc = jnp.dot(y, w_ref[...], preferred_element_type=F32)
    cos = cos_ref[...]
    sin = sin_ref[...]
    for h in range(MLA_HEADS):
        lo = h * HEAD_PAD
        q_ref[:, lo:lo + LANE] = (acc[:, lo:lo + LANE] * Q_SCALE).astype(BF16)
        xr = acc[:, lo + LANE:lo + HEAD_PAD]
        rope = xr * cos + pltpu.roll(xr, LANE // 2, axis=1) * sin
        q_ref[:, lo + LANE:lo + HEAD_PAD] = (rope * Q_SCALE).astype(BF16)


def _q_proj(pa, g, w, cos, sin, *, tm):
    m = pa.shape[0]
    n = w.shape[1]
    return pl.pallas_call(
        _q_proj_kernel,
        grid=(m // tm,),
        in_specs=[pl.BlockSpec((tm, Q_LORA), lambda i: (i, 0)),
                  pl.BlockSpec((1, Q_LORA), lambda i: (0, 0)),
                  pl.BlockSpec((Q_LORA, n), lambda i: (0, 0)),
                  pl.BlockSpec((tm, LANE), lambda i: (i, 0)),
                  pl.BlockSpec((tm, LANE), lambda i: (i, 0))],
        out_specs=pl.BlockSpec((tm, n), lambda i: (i, 0)),
        out_shape=jax.ShapeDtypeStruct((m, n), BF16),
        compiler_params=_cparams(("arbitrary",), 48),
        name="q_proj",
    )(pa, g.reshape(1, Q_LORA), w, cos, sin)


def _kv_proj_kernel(ckv_ref, g_ref, kr_ref, wk_ref, wvt_ref, k_ref, vt_ref):
    y = _rms(ckv_ref[...], g_ref[...]).astype(BF16)
    kn = jnp.dot(y, wk_ref[...], preferred_element_type=F32)
    vt_ref[0] = lax.dot_general(wvt_ref[...], y, (((1,), (1,)), ((), ())),
                                preferred_element_type=F32).astype(BF16)
    kr = kr_ref[...].astype(BF16)
    for h in range(MLA_HEADS):
        lo = h * HEAD_PAD
        k_ref[:, lo:lo + LANE] = kn[:, h * QK_NOPE:(h + 1) * QK_NOPE].astype(BF16)
        k_ref[:, lo + LANE:lo + HEAD_PAD] = kr


def _kv_proj(ckv, g, krot, wk, wvt, *, tm):
    m = ckv.shape[0]
    nv = MLA_HEADS * V_HEAD
    return pl.pallas_call(
        _kv_proj_kernel,
        grid=(m // tm,),
        in_specs=[pl.BlockSpec((tm, KV_LORA), lambda i: (i, 0)),
                  pl.BlockSpec((1, KV_LORA), lambda i: (0, 0)),
                  pl.BlockSpec((tm, LANE), lambda i: (i, 0)),
                  pl.BlockSpec(wk.shape, lambda i: (0, 0)),
                  pl.BlockSpec(wvt.shape, lambda i: (0, 0))],
        out_specs=[pl.BlockSpec((tm, MLA_HEADS * HEAD_PAD), lambda i: (i, 0)),
                   pl.BlockSpec((1, nv, tm), lambda i: (i, 0, 0))],
        out_shape=[jax.ShapeDtypeStruct((m, MLA_HEADS * HEAD_PAD), BF16),
                   jax.ShapeDtypeStruct((m // tm, nv, tm), BF16)],
        compiler_params=_cparams(("arbitrary",), 48),
        name="kv_proj",
    )(ckv, g.reshape(1, KV_LORA), krot, wk, wvt)


def _attn_kernel(q_ref, k_ref, vt_ref, o_ref):
    n_chunks, _, chunk = vt_ref.shape
    q = q_ref[...]
    tq = q.shape[0]
    nt = (((1,), (1,)), ((), ()))
    m = jnp.full((1, tq), NEG, F32)
    l = jnp.zeros((1, tq), F32)
    acc = jnp.zeros((V_HEAD, tq), F32)

    def scores(c):
        return lax.dot_general(k_ref[c * chunk:(c + 1) * chunk, :], q, nt,
                               preferred_element_type=F32)

    s_next = scores(0)
    for c in range(n_chunks):
        s = s_next
        if c + 1 < n_chunks:
            s_next = scores(c + 1)
        m_new = jnp.maximum(m, s.max(axis=0, keepdims=True))
        p = jnp.exp2(s - m_new)
        alpha = jnp.exp2(m - m_new)
        l = alpha * l + p.sum(axis=0, keepdims=True)
        acc = alpha * acc + jnp.dot(vt_ref[c], p.astype(BF16), preferred_element_type=F32)
        m = m_new
    o_ref[...] = (acc / l).T.astype(o_ref.dtype)


def _attention(q, k, vt, *, tq):
    s = q.shape[0]
    kv = k.shape[0]
    n_chunks, _, chunk = vt.shape
    return pl.pallas_call(
        _attn_kernel,
        grid=(MLA_HEADS, s // tq),
        in_specs=[pl.BlockSpec((tq, HEAD_PAD), lambda h, i: (i, h)),
                  pl.BlockSpec((kv, HEAD_PAD), lambda h, i: (0, h)),
                  pl.BlockSpec((n_chunks, V_HEAD, chunk), lambda h, i: (0, h, 0))],
        out_specs=pl.BlockSpec((tq, V_HEAD), lambda h, i: (i, h)),
        out_shape=jax.ShapeDtypeStruct((s, MLA_HEADS * V_HEAD), BF16),
        compiler_params=_cparams(("arbitrary", "arbitrary"), 48),
        name="attention",
    )(q, k, vt)


def _out_proj_kernel(att_ref, conv_ref, wa_ref, wc_ref, x_ref, gate_ref, o_ref):
    acc = jnp.dot(att_ref[...], wa_ref[...], preferred_element_type=F32)
    acc = acc + jnp.dot(conv_ref[...], wc_ref[...], preferred_element_type=F32)
    o_ref[...] = x_ref[...] + gate_ref[...] * acc


def _out_proj(att, conv, wa, wc, x, gate, *, tm, tn):
    m, ka = att.shape
    kc = conv.shape[1]
    n = wa.shape[1]
    return pl.pallas_call(
        _out_proj_kernel,
        grid=(m // tm, n // tn),
        in_specs=[pl.BlockSpec((tm, ka), lambda i, j: (i, 0)),
                  pl.BlockSpec((tm, kc), lambda i, j: (i, 0)),
                  pl.BlockSpec((ka, tn), lambda i, j: (0, j)),
                  pl.BlockSpec((kc, tn), lambda i, j: (0, j)),
                  pl.BlockSpec((tm, tn), lambda i, j: (i, j)),
                  pl.BlockSpec((1, tn), lambda i, j: (0, j))],
        out_specs=pl.BlockSpec((tm, tn), lambda i, j: (i, j)),
        out_shape=jax.ShapeDtypeStruct((m, n), F32),
        compiler_params=_cparams(("arbitrary", "arbitrary"), 48),
        name="out_proj",
    )(att, conv, wa, wc, x, gate.reshape(1, n))


def _top16(pieces, lanes):
    rid = lax.broadcasted_iota(jnp.int32, (PEER_TOPK, lanes), 0)
    tops = jnp.zeros((PEER_TOPK, lanes), F32)
    for k in range(PEER_TOPK):
        m = pieces[0].max(axis=0, keepdims=True)
        for p in pieces[1:]:
            m = jnp.maximum(m, p.max(axis=0, keepdims=True))
        tops = jnp.where(rid == k, m, tops)
        if k + 1 < PEER_TOPK:
            pieces = [jnp.where(p == m, NEG, p) for p in pieces]
    return tops


def _peer_gate_kernel(q_ref, keys_ref, s1_ref, e1_ref, s2_ref, e2_ref, thr_ref):
    lanes = q_ref.shape[0]
    nt = (((1,), (1,)), ((), ()))
    for h in range(PEER_HEADS):
        q1 = q_ref[:, (2 * h) * LANE:(2 * h + 1) * LANE]
        q2 = q_ref[:, (2 * h + 1) * LANE:(2 * h + 2) * LANE]
        s1 = lax.dot_general(keys_ref[2 * h], q1, nt, preferred_element_type=F32)
        s2 = lax.dot_general(keys_ref[2 * h + 1], q2, nt, preferred_element_type=F32)
        t1 = _top16([s1], lanes)
        t2 = _top16([s2], lanes)
        cands = [t1[0:1] + t2]
        cands += [t1[i:i + 1] + t2[0:8] for i in range(1, 8)]
        cands += [t1[8:16] + t2[0:1]]
        top = _top16(cands, lanes)
        z = jnp.sum(jnp.exp(top - top[0:1]), axis=0, keepdims=True)
        s1_ref[h] = s1
        s2_ref[h] = s2
        e1_ref[h] = jnp.exp(s1 - t1[0:1]) / z
        e2_ref[h] = jnp.exp(s2 - t2[0:1])
        thr_ref[h] = top[PEER_TOPK - 1:PEER_TOPK]


def _peer_gates(q, keys, *, tt):
    t = q.shape[0]
    tab = jax.ShapeDtypeStruct((PEER_HEADS, N_KEYS, t), F32)
    tab_spec = pl.BlockSpec((PEER_HEADS, N_KEYS, tt), lambda i: (0, 0, i))
    return pl.pallas_call(
        _peer_gate_kernel,
        grid=(t // tt,),
        in_specs=[pl.BlockSpec((tt, PEER_HEADS * PEER_DK), lambda i: (i, 0)),
                  pl.BlockSpec(keys.shape, lambda i: (0, 0, 0))],
        out_specs=[tab_spec, tab_spec, tab_spec, tab_spec,
                   pl.BlockSpec((PEER_HEADS, 1, tt), lambda i: (0, 0, i))],
        out_shape=[tab, tab, tab, tab, jax.ShapeDtypeStruct((PEER_HEADS, 1, t), F32)],
        compiler_params=_cparams(("arbitrary",), 48),
        name="peer_gates",
    )(q, keys)


def _peer_kernel(ht_ref, u_ref, vt_ref, s1_ref, e1_ref, s2_ref, e2_ref, thr_ref, o_ref,
                 act_scr, w_scr, w_old):
    j = pl.program_id(1)
    te, tm = w_scr.shape

    @pl.when(j == 0)
    def _():
        o_ref[...] = jnp.zeros(o_ref.shape, F32)
        act_scr[...] = jnp.zeros(act_scr.shape, F32)
        w_old[...] = jnp.zeros(w_old.shape, BF16)

    o_ref[...] += jnp.dot(vt_ref[...], w_old[...], preferred_element_type=F32)
    act_new = jnp.dot(u_ref[...], ht_ref[...], preferred_element_type=F32)
    for a in range(te // N_KEYS):
        rows = slice(a * N_KEYS, (a + 1) * N_KEYS)
        for c in range(tm // LANE):
            cols = slice(c * LANE, (c + 1) * LANE)
            act = act_scr[rows, cols]
            gel = 0.5 * act * (1.0 + lax.erf(act * (1.0 / math.sqrt(2.0))))
            g = jnp.zeros((N_KEYS, LANE), F32)
            for h in range(PEER_HEADS):
                s1row = s1_ref[h, 0, a:a + 1, cols]
                e1row = e1_ref[h, 0, a:a + 1, cols]
                sel = jnp.where(s1row + s2_ref[h, :, cols] >= thr_ref[h, :, cols],
                                e2_ref[h, :, cols], 0.0)
                g = g + e1row * sel
            w_scr[rows, cols] = (g * gel).astype(BF16)
    w_old[...] = w_scr[...]
    act_scr[...] = act_new


def _peer_dense(ht, u, vt, tabs, *, tm, te):
    d, t = ht.shape
    ne = u.shape[0]
    n_tiles = ne // te
    s1, e1, s2, e2, thr = tabs
    na = te // N_KEYS
    s1 = s1.reshape(PEER_HEADS, N_KEYS // na, na, t)
    e1 = e1.reshape(PEER_HEADS, N_KEYS // na, na, t)
    tile = lambda j, lag: jnp.clip(j - lag, 0, n_tiles - 1)
    row_spec = pl.BlockSpec((PEER_HEADS, 1, na, tm), lambda i, j: (0, tile(j, 1), 0, i))
    tab_spec = pl.BlockSpec((PEER_HEADS, N_KEYS, tm), lambda i, j: (0, 0, i))
    return pl.pallas_call(
        _peer_kernel,
        grid=(t // tm, n_tiles + 2),
        in_specs=[pl.BlockSpec((d, tm), lambda i, j: (0, i)),
                  pl.BlockSpec((te, d), lambda i, j: (tile(j, 0), 0)),
                  pl.BlockSpec((d, te), lambda i, j: (0, tile(j, 2))),
                  row_spec, row_spec, tab_spec, tab_spec,
                  pl.BlockSpec((PEER_HEADS, 1, tm), lambda i, j: (0, 0, i))],
        out_specs=pl.BlockSpec((d, tm), lambda i, j: (0, i)),
        out_shape=jax.ShapeDtypeStruct((d, t), F32),
        scratch_shapes=[pltpu.VMEM((te, tm), F32), pltpu.VMEM((te, tm), BF16),
                        pltpu.VMEM((te, tm), BF16)],
        compiler_params=_cparams(("arbitrary", "arbitrary"), 60),
        name="peer_dense",
    )(ht, u, vt, s1, e1, s2, e2, thr)


def _final_kernel(pt_ref, x_ref, gate_ref, g_ref, o_ref):
    x2 = x_ref[...] + gate_ref[...] * pt_ref[...].T
    o_ref[...] = _rms(x2, g_ref[...])


def _final(pt, x1, gate, g, *, tf):
    d, t = pt.shape
    return pl.pallas_call(
        _final_kernel,
        grid=(t // tf,),
        in_specs=[pl.BlockSpec((d, tf), lambda i: (0, i)),
                  pl.BlockSpec((tf, d), lambda i: (i, 0)),
                  pl.BlockSpec((1, d), lambda i: (0, 0)),
                  pl.BlockSpec((1, d), lambda i: (0, 0))],
        out_specs=pl.BlockSpec((tf, d), lambda i: (i, 0)),
        out_shape=jax.ShapeDtypeStruct((t, d), F32),
        compiler_params=_cparams(("arbitrary",), 48),
        name="final",
    )(pt, x1, gate.reshape(1, d), g.reshape(1, d))


def _rope_perm():
    j = np.arange(QK_ROPE)
    return np.where((j % 32) < 16, j + 16, j - 16)


def _rope_tables(t):
    rows = t // GRID_W
    row = jnp.repeat(jnp.arange(rows), GRID_W).astype(F32)
    col = jnp.tile(jnp.arange(GRID_W), rows).astype(F32)
    half = QK_ROPE // 2
    freqs = ROPE_THETA ** (-jnp.arange(0, half, 2, dtype=F32) / half)
    ar = row[:, None] * freqs
    ac = col[:, None] * freqs
    zeros = jnp.zeros((t, LANE - QK_ROPE), F32)
    cos = jnp.concatenate([jnp.cos(ar), jnp.cos(ar), jnp.cos(ac), jnp.cos(ac), zeros], axis=1)
    sin = jnp.concatenate([-jnp.sin(ar), jnp.sin(ar), -jnp.sin(ac), jnp.sin(ac), zeros], axis=1)
    return cos, sin


def kernel(x, c, ctx, c_ctx, w_ada, b_ada, norm1_g, w_in, q_norm_g, w_uq, kv_norm_g, w_ukv,
           conv_w, conv_b, w_o, norm2_g, peer_wq, peer_keys, peer_u, peer_v, final_norm_g):
    assert x.shape == (1, SEQ, D_MODEL) and ctx.shape == (1, CTX_LEN, D_MODEL)
    assert w_ada.shape[0] == 1
    x2 = x[0]
    ctx2 = ctx[0]
    perm = _rope_perm()

    w_in0 = w_in[0]
    w_kr_perm = w_in0[:, O_KR:O_CONV][:, perm]
    w_a_x = jnp.concatenate([w_in0[:, :O_CONV], w_kr_perm], axis=1).astype(BF16)
    w_a_c = jnp.concatenate([w_in0[:, O_CKV:O_CONV], w_kr_perm], axis=1).astype(BF16)
    w_conv = w_in0[:, O_CONV:].astype(BF16)
    wq = w_uq[0].reshape(Q_LORA, MLA_HEADS, QK_NOPE + QK_ROPE)
    wq = jnp.concatenate([wq, wq[:, :, QK_NOPE:][:, :, perm]], axis=2)
    wq = wq.reshape(Q_LORA, MLA_HEADS * HEAD_PAD).astype(BF16)
    wkv = w_ukv[0].reshape(KV_LORA, MLA_HEADS, QK_NOPE + V_HEAD)
    wk = wkv[:, :, :QK_NOPE].reshape(KV_LORA, MLA_HEADS * QK_NOPE).astype(BF16)
    wvt = wkv[:, :, QK_NOPE:].reshape(KV_LORA, MLA_HEADS * V_HEAD).T.astype(BF16)
    wo_att = w_o[0, :MLA_HEADS * V_HEAD].astype(BF16)
    wo_conv = w_o[0, MLA_HEADS * V_HEAD:].astype(BF16)
    w_pq = peer_wq[0].astype(BF16)
    keys = peer_keys[0].reshape(PEER_HEADS * 2, N_KEYS, PEER_DK // 2).astype(BF16)
    u_bf = peer_u[0].astype(BF16)
    vt_bf = peer_v[0].astype(BF16).T

    cos_x, sin_x = _rope_tables(SEQ)
    cos_c = jnp.concatenate([jnp.ones((CTX_LEN, QK_ROPE), F32),
                             jnp.zeros((CTX_LEN, LANE - QK_ROPE), F32)], axis=1)
    sin_c = jnp.zeros((CTX_LEN, LANE), F32)

    cond8 = jnp.concatenate([c, c_ctx[None, :], jnp.zeros((6, D_MODEL), F32)], axis=0)
    mods = _mods(cond8, w_ada[0], b_ada[0])
    m_x = mods[0].reshape(N_MOD, D_MODEL)
    m_c = mods[1].reshape(N_MOD, D_MODEL)

    pa_x, hx = _norm_proj(x2, norm1_g[0], m_x[0], m_x[1], w_a_x, tm=256, tn=w_a_x.shape[1],
                          out_dtype=F32, cos=cos_x, sin=sin_x)
    pa_c, _ = _norm_proj(ctx2, norm1_g[0], m_c[0], m_c[1], w_a_c, tm=CTX_LEN, tn=w_a_c.shape[1],
                         out_dtype=F32, cos=cos_c, sin=sin_c)
    pconv = _matmul(hx, w_conv, tm=1024, tn=1024, out_dtype=F32)
    conv = _short_conv(pconv, conv_w[0], conv_b[0])

    q = _q_proj(pa_x, q_norm_g[0], wq, cos_x, sin_x, tm=256)
    ckv = jnp.concatenate([pa_x[:, O_CKV:O_KR], pa_c[:, :KV_LORA]], axis=0)
    krot = jnp.concatenate([pa_x[:, O_KR:], pa_c[:, KV_LORA:]], axis=0)
    k, vt = _kv_proj(ckv, kv_norm_g[0], krot, wk, wvt, tm=KV_CHUNK)
    att = _attention(q, k, vt, tq=512)

    x1 = _out_proj(att, conv, wo_att, wo_conv, x2, m_x[2], tm=1024, tn=512)

    qp, h2t = _norm_proj(x1, norm2_g[0], m_x[3], m_x[4], w_pq, tm=512, tn=1024,
                         out_dtype=BF16, transpose_h=True)
    tabs = _peer_gates(qp, keys, tt=512)
    pt = _peer_dense(h2t, u_bf, vt_bf, tabs, tm=512, te=512)
    out = _final(pt, x1, m_x[5], final_norm_g, tf=256)
    return out[None]
```

```python
import functools
import math

import numpy as np
import jax
import jax.numpy as jnp
from jax import lax
from jax.experimental import pallas as pl
from jax.experimental.pallas import tpu as pltpu

F32 = jnp.float32
BF16 = jnp.bfloat16

D_MODEL = 4096
SEQ = 8192
GRID_W = 64
CTX_LEN = 256
MLA_HEADS = 16
QK_NOPE = 128
QK_ROPE = 64
V_HEAD = 128
Q_LORA = 768
KV_LORA = 512
ROPE_THETA = 10000.0
SOFTMAX_SCALE = 1.0 / math.sqrt(QK_NOPE + QK_ROPE)
CONV_CH = 2048
PEER_HEADS = 8
N_KEYS = 128
N_EXPERTS = N_KEYS * N_KEYS
PEER_TOPK = 16
PEER_DK = 256
N_MOD = 6
EPS = 1e-6
O_CKV = Q_LORA
O_KR = Q_LORA + KV_LORA
O_CONV = O_KR + QK_ROPE

LANE = 128
HEAD_PAD = 256
KV_LEN = SEQ + CTX_LEN
KV_CHUNK = 768
Q_SCALE = SOFTMAX_SCALE * math.log2(math.e)
PEER_TE = 512
NEG = float(np.finfo(np.float32).min)
MIB = 1024 * 1024


def _cparams(sem, vmem_mib):
    return pltpu.CompilerParams(dimension_semantics=sem, vmem_limit_bytes=vmem_mib * MIB)


def _rms(x, g):
    ms = jnp.mean(x * x, axis=-1, keepdims=True)
    return x * lax.rsqrt(ms + EPS) * g


def _mods_kernel(cond_ref, w_ref, b_ref, o_ref):
    c = cond_ref[...]
    s = c * (1.0 / (1.0 + jnp.exp(-c)))
    o_ref[...] = jnp.dot(s.astype(BF16), w_ref[...].astype(BF16),
                         preferred_element_type=F32) + b_ref[...]


def _mods(cond8, w, b):
    n = w.shape[1]
    tn = 512
    return pl.pallas_call(
        _mods_kernel,
        grid=(n // tn,),
        in_specs=[pl.BlockSpec((8, D_MODEL), lambda j: (0, 0)),
                  pl.BlockSpec((D_MODEL, tn), lambda j: (0, j)),
                  pl.BlockSpec((1, tn), lambda j: (0, j))],
        out_specs=pl.BlockSpec((8, tn), lambda j: (0, j)),
        out_shape=jax.ShapeDtypeStruct((8, n), F32),
        compiler_params=_cparams(("arbitrary",), 48),
        name="mods",
    )(cond8, w, b.reshape(1, n))


def _norm_proj_kernel(*refs, rope_last, transpose_h, row_chunk):
    if rope_last:
        x_ref, g_ref, sh_ref, sc_ref, w_ref, cos_ref, sin_ref, o_ref, h_ref, h_scr = refs
    else:
        x_ref, g_ref, sh_ref, sc_ref, w_ref, o_ref, h_ref, h_scr = refs
    tm = x_ref.shape[0]

    @pl.when(pl.program_id(1) == 0)
    def _():
        for r in range(tm // row_chunk):
            rows = slice(r * row_chunk, (r + 1) * row_chunk)
            y = _rms(x_ref[rows, :], g_ref[...])
            h = y * (1.0 + sc_ref[...]) + sh_ref[...]
            hb = h.astype(BF16)
            h_scr[rows, :] = hb
            if transpose_h:
                h_ref[0, :, rows] = h.T.astype(BF16)
            else:
                h_ref[rows, :] = hb

    acc = jnp.dot(h_scr[...], w_ref[...], preferred_element_type=F32)
    if rope_last:
        n = acc.shape[1]
        o_ref[:, : n - LANE] = acc[:, : n - LANE].astype(o_ref.dtype)
        xr = acc[:, n - LANE:]
        o_ref[:, n - LANE:] = (xr * cos_ref[...]
                               + pltpu.roll(xr, LANE // 2, axis=1) * sin_ref[...]).astype(o_ref.dtype)
    else:
        o_ref[...] = acc.astype(o_ref.dtype)


def _norm_proj(x, g, shift, scale, w, *, tm, tn, out_dtype, cos=None, sin=None,
               transpose_h=False, vmem_mib=56):
    m, k = x.shape
    n = w.shape[1]
    rope_last = cos is not None
    row = lambda a: a.reshape(1, k)
    in_specs = [pl.BlockSpec((tm, k), lambda i, j: (i, 0)),
                pl.BlockSpec((1, k), lambda i, j: (0, 0)),
                pl.BlockSpec((1, k), lambda i, j: (0, 0)),
                pl.BlockSpec((1, k), lambda i, j: (0, 0)),
                pl.BlockSpec((k, tn), lambda i, j: (0, j))]
    args = [x, row(g), row(shift), row(scale), w]
    if rope_last:
        assert tn == n
        in_specs += [pl.BlockSpec((tm, LANE), lambda i, j: (i, 0)),
                     pl.BlockSpec((tm, LANE), lambda i, j: (i, 0))]
        args += [cos, sin]
    if transpose_h:
        h_shape, h_spec = (m // tm, k, tm), pl.BlockSpec((1, k, tm), lambda i, j: (i, 0, 0))
    else:
        h_shape, h_spec = (m, k), pl.BlockSpec((tm, k), lambda i, j: (i, 0))
    return pl.pallas_call(
        functools.partial(_norm_proj_kernel, rope_last=rope_last, transpose_h=transpose_h,
                          row_chunk=min(tm, LANE)),
        grid=(m // tm, n // tn),
        in_specs=in_specs,
        out_specs=[pl.BlockSpec((tm, tn), lambda i, j: (i, j)), h_spec],
        out_shape=[jax.ShapeDtypeStruct((m, n), out_dtype),
                   jax.ShapeDtypeStruct(h_shape, BF16)],
        scratch_shapes=[pltpu.VMEM((tm, k), BF16)],
        compiler_params=_cparams(("arbitrary", "arbitrary"), vmem_mib),
        name="norm_proj",
    )(*args)


def _matmul_kernel(a_ref, w_ref, o_ref):
    o_ref[...] = jnp.dot(a_ref[...], w_ref[...], preferred_element_type=F32).astype(o_ref.dtype)


def _matmul(a, w, *, tm, tn, out_dtype, vmem_mib=48):
    m, k = a.shape
    n = w.shape[1]
    return pl.pallas_call(
        _matmul_kernel,
        grid=(m // tm, n // tn),
        in_specs=[pl.BlockSpec((tm, k), lambda i, j: (i, 0)),
                  pl.BlockSpec((k, tn), lambda i, j: (0, j))],
        out_specs=pl.BlockSpec((tm, tn), lambda i, j: (i, j)),
        out_shape=jax.ShapeDtypeStruct((m, n), out_dtype),
        compiler_params=_cparams(("arbitrary", "arbitrary"), vmem_mib),
        name="matmul",
    )(a, w)


def _conv_kernel(xin_ref, gb_ref, gc_ref, w_ref, b_ref, o_ref):
    t = xin_ref.shape[0]
    u = gc_ref[...] * xin_ref[...]
    row = lax.broadcasted_iota(jnp.int32, u.shape, 0)
    prev = jnp.where(row == 0, 0.0, pltpu.roll(u, 1, axis=0))
    nxt = jnp.where(row == t - 1, 0.0, pltpu.roll(u, t - 1, axis=0))
    y = prev * w_ref[0:1, :] + u * w_ref[1:2, :] + nxt * w_ref[2:3, :] + b_ref[...]
    o_ref[...] = (gb_ref[...] * y).astype(o_ref.dtype)


def _short_conv(pconv, w, b):
    t = pconv.shape[0]
    nb = CONV_CH // LANE
    return pl.pallas_call(
        _conv_kernel,
        grid=(nb,),
        in_specs=[pl.BlockSpec((t, LANE), lambda j: (0, j)),
                  pl.BlockSpec((t, LANE), lambda j: (0, nb + j)),
                  pl.BlockSpec((t, LANE), lambda j: (0, 2 * nb + j)),
                  pl.BlockSpec((3, LANE), lambda j: (0, j)),
                  pl.BlockSpec((1, LANE), lambda j: (0, j))],
        out_specs=pl.BlockSpec((t, LANE), lambda j: (0, j)),
        out_shape=jax.ShapeDtypeStruct((t, CONV_CH), BF16),
        compiler_params=_cparams(("arbitrary",), 56),
        name="short_conv",
    )(pconv, pconv, pconv, w, b.reshape(1, CONV_CH))


def _q_proj_kernel(cq_ref, g_ref, w_ref, cos_ref, sin_ref, q_ref):
    y = _rms(cq_ref[...], g_ref[...]).astype(BF16)
    acc = jnp.dot(y, w_ref[...], preferred_element_type=F32)
    cos = cos_ref[...]
    sin = sin_ref[...]
    for h in range(MLA_HEADS):
        lo = h * HEAD_PAD
        q_ref[h, :, :LANE] = (acc[:, lo:lo + LANE] * Q_SCALE).astype(BF16)
        xr = acc[:, lo + LANE:lo + HEAD_PAD]
        rope = xr * cos + pltpu.roll(xr, LANE // 2, axis=1) * sin
        q_ref[h, :, LANE:] = (rope * Q_SCALE).astype(BF16)


def _q_proj(pa, g, w, cos, sin, *, tm):
    m = pa.shape[0]
    n = w.shape[1]
    return pl.pallas_call(
        _q_proj_kernel,
        grid=(m // tm,),
        in_specs=[pl.BlockSpec((tm, Q_LORA), lambda i: (i, 0)),
                  pl.BlockSpec((1, Q_LORA), lambda i: (0, 0)),
                  pl.BlockSpec((Q_LORA, n), lambda i: (0, 0)),
                  pl.BlockSpec((tm, LANE), lambda i: (i, 0)),
                  pl.BlockSpec((tm, LANE), lambda i: (i, 0))],
        out_specs=pl.BlockSpec((MLA_HEADS, tm, HEAD_PAD), lambda i: (0, i, 0)),
        out_shape=jax.ShapeDtypeStruct((MLA_HEADS, m, HEAD_PAD), BF16),
        compiler_params=_cparams(("arbitrary",), 48),
        name="q_proj",
    )(pa, g.reshape(1, Q_LORA), w, cos, sin)


def _kv_proj_kernel(ckv_ref, g_ref, kr_ref, wk_ref, wvt_ref, k_ref, vt_ref):
    y = _rms(ckv_ref[...], g_ref[...]).astype(BF16)
    kn = jnp.dot(y, wk_ref[...], preferred_element_type=F32)
    vt_ref[0] = lax.dot_general(wvt_ref[...], y, (((1,), (1,)), ((), ())),
                                preferred_element_type=F32).astype(BF16)
    kr = kr_ref[...].astype(BF16)
    for h in range(MLA_HEADS):
        k_ref[h, :, :LANE] = kn[:, h * QK_NOPE:(h + 1) * QK_NOPE].astype(BF16)
        k_ref[h, :, LANE:] = kr


def _kv_proj(ckv, g, krot, wk, wvt, *, tm):
    m = ckv.shape[0]
    nv = MLA_HEADS * V_HEAD
    return pl.pallas_call(
        _kv_proj_kernel,
        grid=(m // tm,),
        in_specs=[pl.BlockSpec((tm, KV_LORA), lambda i: (i, 0)),
                  pl.BlockSpec((1, KV_LORA), lambda i: (0, 0)),
                  pl.BlockSpec((tm, LANE), lambda i: (i, 0)),
                  pl.BlockSpec(wk.shape, lambda i: (0, 0)),
                  pl.BlockSpec(wvt.shape, lambda i: (0, 0))],
        out_specs=[pl.BlockSpec((MLA_HEADS, tm, HEAD_PAD), lambda i: (0, i, 0)),
                   pl.BlockSpec((1, nv, tm), lambda i: (i, 0, 0))],
        out_shape=[jax.ShapeDtypeStruct((MLA_HEADS, m, HEAD_PAD), BF16),
                   jax.ShapeDtypeStruct((m // tm, nv, tm), BF16)],
        compiler_params=_cparams(("arbitrary",), 48),
        name="kv_proj",
    )(ckv, g.reshape(1, KV_LORA), krot, wk, wvt)


def _attn_kernel(q_ref, k_ref, vt_ref, o_ref):
    n_chunks, _, chunk = vt_ref.shape
    q = q_ref[0]
    tq = q.shape[0]
    nt = (((1,), (1,)), ((), ()))
    m = jnp.full((1, tq), NEG, F32)
    l = jnp.zeros((1, tq), F32)
    acc = jnp.zeros((V_HEAD, tq), F32)

    def scores(c):
        return lax.dot_general(k_ref[0, c * chunk:(c + 1) * chunk, :], q, nt,
                               preferred_element_type=F32)

    s_next = scores(0)
    for c in range(n_chunks):
        s = s_next
        if c + 1 < n_chunks:
            s_next = scores(c + 1)
        m_new = jnp.maximum(m, s.max(axis=0, keepdims=True))
        p = jnp.exp2(s - m_new)
        alpha = jnp.exp2(m - m_new)
        l = alpha * l + p.sum(axis=0, keepdims=True)
        acc = alpha * acc + jnp.dot(vt_ref[c], p.astype(BF16), preferred_element_type=F32)
        m = m_new
    o_ref[...] = (acc / l).T.astype(o_ref.dtype)


def _attention(q, k, vt, *, tq):
    s = q.shape[1]
    kv = k.shape[1]
    n_chunks, _, chunk = vt.shape
    return pl.pallas_call(
        _attn_kernel,
        grid=(MLA_HEADS, s // tq),
        in_specs=[pl.BlockSpec((1, tq, HEAD_PAD), lambda h, i: (h, i, 0)),
                  pl.BlockSpec((1, kv, HEAD_PAD), lambda h, i: (h, 0, 0)),
                  pl.BlockSpec((n_chunks, V_HEAD, chunk), lambda h, i: (0, h, 0))],
        out_specs=pl.BlockSpec((tq, V_HEAD), lambda h, i: (i, h)),
        out_shape=jax.ShapeDtypeStruct((s, MLA_HEADS * V_HEAD), BF16),
        compiler_params=_cparams(("arbitrary", "arbitrary"), 48),
        name="attention",
    )(q, k, vt)


def _out_proj_kernel(att_ref, conv_ref, wa_ref, wc_ref, x_ref, gate_ref, o_ref):
    acc = jnp.dot(att_ref[...], wa_ref[...], preferred_element_type=F32)
    acc = acc + jnp.dot(conv_ref[...], wc_ref[...], preferred_element_type=F32)
    o_ref[...] = x_ref[...] + gate_ref[...] * acc


def _out_proj(att, conv, wa, wc, x, gate, *, tm, tn):
    m, ka = att.shape
    kc = conv.shape[1]
    n = wa.shape[1]
    return pl.pallas_call(
        _out_proj_kernel,
        grid=(m // tm, n // tn),
        in_specs=[pl.BlockSpec((tm, ka), lambda i, j: (i, 0)),
                  pl.BlockSpec((tm, kc), lambda i, j: (i, 0)),
                  pl.BlockSpec((ka, tn), lambda i, j: (0, j)),
                  pl.BlockSpec((kc, tn), lambda i, j: (0, j)),
                  pl.BlockSpec((tm, tn), lambda i, j: (i, j)),
                  pl.BlockSpec((1, tn), lambda i, j: (0, j))],
        out_specs=pl.BlockSpec((tm, tn), lambda i, j: (i, j)),
        out_shape=jax.ShapeDtypeStruct((m, n), F32),
        compiler_params=_cparams(("arbitrary", "arbitrary"), 48),
        name="out_proj",
    )(att, conv, wa, wc, x, gate.reshape(1, n))


def _top16(pieces, lanes):
    rid = lax.broadcasted_iota(jnp.int32, (PEER_TOPK, lanes), 0)
    tops = jnp.zeros((PEER_TOPK, lanes), F32)
    for k in range(PEER_TOPK):
        m = pieces[0].max(axis=0, keepdims=True)
        for p in pieces[1:]:
            m = jnp.maximum(m, p.max(axis=0, keepdims=True))
        tops = jnp.where(rid == k, m, tops)
        if k + 1 < PEER_TOPK:
            pieces = [jnp.where(p == m, NEG, p) for p in pieces]
    return tops


def _peer_gate_kernel(q_ref, keys_ref, s1_ref, e1_ref, s2_ref, e2_ref, thr_ref):
    lanes = q_ref.shape[0]
    n_groups, na = s1_ref.shape[1], s1_ref.shape[3]
    nt = (((1,), (1,)), ((), ()))
    for h in range(PEER_HEADS):
        q1 = q_ref[:, (2 * h) * LANE:(2 * h + 1) * LANE]
        q2 = q_ref[:, (2 * h + 1) * LANE:(2 * h + 2) * LANE]
        s1 = lax.dot_general(keys_ref[2 * h], q1, nt, preferred_element_type=F32)
        s2 = lax.dot_general(keys_ref[2 * h + 1], q2, nt, preferred_element_type=F32)
        t1 = _top16([s1], lanes)
        t2 = _top16([s2], lanes)
        cands = [t1[0:1] + t2]
        cands += [t1[i:i + 1] + t2[0:8] for i in range(1, 8)]
        cands += [t1[8:16] + t2[0:1]]
        top = _top16(cands, lanes)
        z = jnp.sum(jnp.exp(top - top[0:1]), axis=0, keepdims=True)
        e1 = jnp.exp(s1 - t1[0:1]) / z
        for r in range(n_groups):
            s1_ref[0, r, h] = s1[r * na:(r + 1) * na, :]
            e1_ref[0, r, h] = e1[r * na:(r + 1) * na, :]
        s2_ref[0, h] = s2
        e2_ref[0, h] = jnp.exp(s2 - t2[0:1])
        thr_ref[0, h] = top[PEER_TOPK - 1:PEER_TOPK]


def _peer_gates(q, keys, *, tt, na):
    t = q.shape[0]
    nb = t // tt
    row_shape = (nb, N_KEYS // na, PEER_HEADS, na, tt)
    tab_shape = (nb, PEER_HEADS, N_KEYS, tt)
    thr_shape = (nb, PEER_HEADS, 1, tt)
    blk = lambda shape: pl.BlockSpec((1,) + shape[1:], lambda i: (i,) + (0,) * (len(shape) - 1))
    return pl.pallas_call(
        _peer_gate_kernel,
        grid=(nb,),
        in_specs=[pl.BlockSpec((tt, PEER_HEADS * PEER_DK), lambda i: (i, 0)),
                  pl.BlockSpec(keys.shape, lambda i: (0, 0, 0))],
        out_specs=[blk(row_shape), blk(row_shape), blk(tab_shape), blk(tab_shape),
                   blk(thr_shape)],
        out_shape=[jax.ShapeDtypeStruct(s, F32) for s in
                   (row_shape, row_shape, tab_shape, tab_shape, thr_shape)],
        compiler_params=_cparams(("arbitrary",), 48),
        name="peer_gates",
    )(q, keys)


def _peer_kernel(ht_ref, u_ref, vt_ref, s1_ref, e1_ref, s2_ref, e2_ref, thr_ref, o_ref,
                 act_scr, w_scr, w_old):
    j = pl.program_id(1)
    te, tm = w_scr.shape

    @pl.when(j == 0)
    def _():
        o_ref[...] = jnp.zeros(o_ref.shape, F32)
        act_scr[...] = jnp.zeros(act_scr.shape, F32)
        w_old[...] = jnp.zeros(w_old.shape, BF16)

    o_ref[0] += jnp.dot(vt_ref[0], w_old[...], preferred_element_type=F32)
    act_new = jnp.dot(u_ref[...], ht_ref[0], preferred_element_type=F32)
    for a in range(te // N_KEYS):
        rows = slice(a * N_KEYS, (a + 1) * N_KEYS)
        for c in range(tm // LANE):
            cols = slice(c * LANE, (c + 1) * LANE)
            act = act_scr[rows, cols]
            gel = 0.5 * act * (1.0 + lax.erf(act * (1.0 / math.sqrt(2.0))))
            g = jnp.zeros((N_KEYS, LANE), F32)
            for h in range(PEER_HEADS):
                s1row = s1_ref[0, 0, h, a:a + 1, cols]
                e1row = e1_ref[0, 0, h, a:a + 1, cols]
                sel = jnp.where(s1row + s2_ref[0, h, :, cols] >= thr_ref[0, h, :, cols],
                                e2_ref[0, h, :, cols], 0.0)
                g = g + e1row * sel
            w_scr[rows, cols] = (g * gel).astype(BF16)
    w_old[...] = w_scr[...]
    act_scr[...] = act_new


def _peer_dense(ht, u, vt, tabs):
    nb, d, tm = ht.shape
    n_tiles, _, te = vt.shape
    s1, e1, s2, e2, thr = tabs
    na = te // N_KEYS
    assert s1.shape == (nb, N_KEYS // na, PEER_HEADS, na, tm)
    tile = lambda j, lag: jnp.clip(j - lag, 0, n_tiles - 1)
    row_spec = pl.BlockSpec((1, 1, PEER_HEADS, na, tm), lambda i, j: (i, tile(j, 1), 0, 0, 0))
    tab_spec = pl.BlockSpec((1, PEER_HEADS, N_KEYS, tm), lambda i, j: (i, 0, 0, 0))
    return pl.pallas_call(
        _peer_kernel,
        grid=(nb, n_tiles + 2),
        in_specs=[pl.BlockSpec((1, d, tm), lambda i, j: (i, 0, 0)),
                  pl.BlockSpec((te, d), lambda i, j: (tile(j, 0), 0)),
                  pl.BlockSpec((1, d, te), lambda i, j: (tile(j, 2), 0, 0)),
                  row_spec, row_spec, tab_spec, tab_spec,
                  pl.BlockSpec((1, PEER_HEADS, 1, tm), lambda i, j: (i, 0, 0, 0))],
        out_specs=pl.BlockSpec((1, d, tm), lambda i, j: (i, 0, 0)),
        out_shape=jax.ShapeDtypeStruct((nb, d, tm), F32),
        scratch_shapes=[pltpu.VMEM((te, tm), F32), pltpu.VMEM((te, tm), BF16),
                        pltpu.VMEM((te, tm), BF16)],
        compiler_params=_cparams(("arbitrary", "arbitrary"), 60),
        name="peer_dense",
    )(ht, u, vt, s1, e1, s2, e2, thr)


def _final_kernel(pt_ref, x_ref, gate_ref, g_ref, o_ref):
    x2 = x_ref[...] + gate_ref[...] * pt_ref[0].T
    o_ref[...] = _rms(x2, g_ref[...])


def _final(pt, x1, gate, g, *, tf):
    nb, d, tm = pt.shape
    t = nb * tm
    per = tm // tf
    return pl.pallas_call(
        _final_kernel,
        grid=(t // tf,),
        in_specs=[pl.BlockSpec((1, d, tf), lambda i: (i // per, 0, i % per)),
                  pl.BlockSpec((tf, d), lambda i: (i, 0)),
                  pl.BlockSpec((1, d), lambda i: (0, 0)),
                  pl.BlockSpec((1, d), lambda i: (0, 0))],
        out_specs=pl.BlockSpec((tf, d), lambda i: (i, 0)),
        out_shape=jax.ShapeDtypeStruct((t, d), F32),
        compiler_params=_cparams(("arbitrary",), 48),
        name="final",
    )(pt, x1, gate.reshape(1, d), g.reshape(1, d))


def _rope_perm():
    j = np.arange(QK_ROPE)
    return np.where((j % 32) < 16, j + 16, j - 16)


def _rope_tables(t):
    f32 = np.float32
    rows = t // GRID_W
    row = np.repeat(np.arange(rows), GRID_W).astype(f32)
    col = np.tile(np.arange(GRID_W), rows).astype(f32)
    half = QK_ROPE // 2
    freqs = (f32(ROPE_THETA) ** (-np.arange(0, half, 2, dtype=f32) / f32(half))).astype(f32)
    ar = (row[:, None] * freqs).astype(np.float64)
    ac = (col[:, None] * freqs).astype(np.float64)
    zeros = np.zeros((t, LANE - QK_ROPE))
    cos = np.concatenate([np.cos(ar), np.cos(ar), np.cos(ac), np.cos(ac), zeros], axis=1)
    sin = np.concatenate([-np.sin(ar), np.sin(ar), -np.sin(ac), np.sin(ac), zeros], axis=1)
    return jnp.asarray(cos.astype(f32)), jnp.asarray(sin.astype(f32))


def kernel(x, c, ctx, c_ctx, w_ada, b_ada, norm1_g, w_in, q_norm_g, w_uq, kv_norm_g, w_ukv,
           conv_w, conv_b, w_o, norm2_g, peer_wq, peer_keys, peer_u, peer_v, final_norm_g):
    assert x.shape == (1, SEQ, D_MODEL) and ctx.shape == (1, CTX_LEN, D_MODEL)
    assert w_ada.shape[0] == 1
    x2 = x[0]
    ctx2 = ctx[0]
    perm = _rope_perm()

    w_in0 = w_in[0]
    w_kr_perm = w_in0[:, O_KR:O_CONV][:, perm]
    w_a_x = jnp.concatenate([w_in0[:, :O_CONV], w_kr_perm], axis=1).astype(BF16)
    w_a_c = jnp.concatenate([w_in0[:, O_CKV:O_CONV], w_kr_perm], axis=1).astype(BF16)
    w_conv = w_in0[:, O_CONV:].astype(BF16)
    wq = w_uq[0].reshape(Q_LORA, MLA_HEADS, QK_NOPE + QK_ROPE)
    wq = jnp.concatenate([wq, wq[:, :, QK_NOPE:][:, :, perm]], axis=2)
    wq = wq.reshape(Q_LORA, MLA_HEADS * HEAD_PAD).astype(BF16)
    wkv = w_ukv[0].reshape(KV_LORA, MLA_HEADS, QK_NOPE + V_HEAD)
    wk = wkv[:, :, :QK_NOPE].reshape(KV_LORA, MLA_HEADS * QK_NOPE).astype(BF16)
    wvt = wkv[:, :, QK_NOPE:].reshape(KV_LORA, MLA_HEADS * V_HEAD).T.astype(BF16)
    wo_att = w_o[0, :MLA_HEADS * V_HEAD].astype(BF16)
    wo_conv = w_o[0, MLA_HEADS * V_HEAD:].astype(BF16)
    w_pq = peer_wq[0].astype(BF16)
    keys = peer_keys[0].reshape(PEER_HEADS * 2, N_KEYS, PEER_DK // 2).astype(BF16)
    u_bf = peer_u[0].astype(BF16)
    vt_bf = peer_v[0].astype(BF16).reshape(N_EXPERTS // PEER_TE, PEER_TE, D_MODEL)
    vt_bf = vt_bf.transpose(0, 2, 1)

    cos_x, sin_x = _rope_tables(SEQ)
    cos_c = jnp.concatenate([jnp.ones((CTX_LEN, QK_ROPE), F32),
                             jnp.zeros((CTX_LEN, LANE - QK_ROPE), F32)], axis=1)
    sin_c = jnp.zeros((CTX_LEN, LANE), F32)

    cond8 = jnp.concatenate([c, c_ctx[None, :], jnp.zeros((6, D_MODEL), F32)], axis=0)
    mods = _mods(cond8, w_ada[0], b_ada[0])
    m_x = mods[0].reshape(N_MOD, D_MODEL)
    m_c = mods[1].reshape(N_MOD, D_MODEL)

    pa_x, hx = _norm_proj(x2, norm1_g[0], m_x[0], m_x[1], w_a_x, tm=256, tn=w_a_x.shape[1],
                          out_dtype=F32, cos=cos_x, sin=sin_x)
    pa_c, _ = _norm_proj(ctx2, norm1_g[0], m_c[0], m_c[1], w_a_c, tm=CTX_LEN, tn=w_a_c.shape[1],
                         out_dtype=F32, cos=cos_c, sin=sin_c)
    pconv = _matmul(hx, w_conv, tm=1024, tn=1024, out_dtype=F32)
    conv = _short_conv(pconv, conv_w[0], conv_b[0])

    q = _q_proj(pa_x, q_norm_g[0], wq, cos_x, sin_x, tm=256)
    ckv = jnp.concatenate([pa_x[:, O_CKV:O_KR], pa_c[:, :KV_LORA]], axis=0)
    krot = jnp.concatenate([pa_x[:, O_KR:], pa_c[:, KV_LORA:]], axis=0)
    k, vt = _kv_proj(ckv, kv_norm_g[0], krot, wk, wvt, tm=KV_CHUNK)
    att = _attention(q, k, vt, tq=512)

    x1 = _out_proj(att, conv, wo_att, wo_conv, x2, m_x[2], tm=1024, tn=512)

    qp, h2t = _norm_proj(x1, norm2_g[0], m_x[3], m_x[4], w_pq, tm=512, tn=1024,
                         out_dtype=BF16, transpose_h=True)
    tabs = _peer_gates(qp, keys, tt=512, na=PEER_TE // N_KEYS)
    peer_t = _peer_dense(h2t, u_bf, vt_bf, tabs)
    out = _final(peer_t, x1, m_x[5], final_norm_g, tf=256)
    return out[None]
```

```python
import functools
import math

import numpy as np
import jax
import jax.numpy as jnp
from jax import lax
from jax.experimental import pallas as pl
from jax.experimental.pallas import tpu as pltpu

F32 = jnp.float32
BF16 = jnp.bfloat16

D_MODEL = 4096
SEQ = 8192
GRID_W = 64
CTX_LEN = 256
MLA_HEADS = 16
QK_NOPE = 128
QK_ROPE = 64
V_HEAD = 128
Q_LORA = 768
KV_LORA = 512
ROPE_THETA = 10000.0
SOFTMAX_SCALE = 1.0 / math.sqrt(QK_NOPE + QK_ROPE)
CONV_CH = 2048
PEER_HEADS = 8
N_KEYS = 128
N_EXPERTS = N_KEYS * N_KEYS
PEER_TOPK = 16
PEER_DK = 256
N_MOD = 6
EPS = 1e-6
O_CKV = Q_LORA
O_KR = Q_LORA + KV_LORA
O_CONV = O_KR + QK_ROPE

LANE = 128
HEAD_PAD = 256
KV_LEN = SEQ + CTX_LEN
KV_CHUNK = 1408
Q_SCALE = SOFTMAX_SCALE * math.log2(math.e)
QK_AHEAD = 2
PEER_TE = 512
PEER_HALF = 256
NEG = float(np.finfo(np.float32).min)
MIB = 1024 * 1024


def _cparams(sem, vmem_mib):
    return pltpu.CompilerParams(dimension_semantics=sem, vmem_limit_bytes=vmem_mib * MIB)


def _rms(x, g):
    ms = jnp.mean(x * x, axis=-1, keepdims=True)
    return x * lax.rsqrt(ms + EPS) * g


def _mods_kernel(cond_ref, w_ref, b_ref, o_ref):
    c = cond_ref[...]
    s = c * (1.0 / (1.0 + jnp.exp(-c)))
    o_ref[...] = jnp.dot(s.astype(BF16), w_ref[...].astype(BF16),
                         preferred_element_type=F32) + b_ref[...]


def _mods(cond8, w, b):
    n = w.shape[1]
    tn = 512
    return pl.pallas_call(
        _mods_kernel,
        grid=(n // tn,),
        in_specs=[pl.BlockSpec((8, D_MODEL), lambda j: (0, 0)),
                  pl.BlockSpec((D_MODEL, tn), lambda j: (0, j)),
                  pl.BlockSpec((1, tn), lambda j: (0, j))],
        out_specs=pl.BlockSpec((8, tn), lambda j: (0, j)),
        out_shape=jax.ShapeDtypeStruct((8, n), F32),
        compiler_params=_cparams(("arbitrary",), 48),
        name="mods",
    )(cond8, w, b.reshape(1, n))


def _norm_proj_kernel(*refs, rope_last, transpose_h, row_chunk):
    if rope_last:
        x_ref, g_ref, sh_ref, sc_ref, w_ref, cos_ref, sin_ref, o_ref, h_ref, h_scr = refs
    else:
        x_ref, g_ref, sh_ref, sc_ref, w_ref, o_ref, h_ref, h_scr = refs
    tm = x_ref.shape[0]

    @pl.when(pl.program_id(1) == 0)
    def _():
        for r in range(tm // row_chunk):
            rows = slice(r * row_chunk, (r + 1) * row_chunk)
            y = _rms(x_ref[rows, :], g_ref[...])
            h = y * (1.0 + sc_ref[...]) + sh_ref[...]
            hb = h.astype(BF16)
            h_scr[rows, :] = hb
            if transpose_h:
                h_ref[0, :, rows] = h.T.astype(BF16)
            else:
                h_ref[rows, :] = hb

    acc = jnp.dot(h_scr[...], w_ref[...], preferred_element_type=F32)
    if rope_last:
        n = acc.shape[1]
        o_ref[:, : n - LANE] = acc[:, : n - LANE].astype(o_ref.dtype)
        xr = acc[:, n - LANE:]
        o_ref[:, n - LANE:] = (xr * cos_ref[...]
                               + pltpu.roll(xr, LANE // 2, axis=1) * sin_ref[...]).astype(o_ref.dtype)
    else:
        o_ref[...] = acc.astype(o_ref.dtype)


def _norm_proj(x, g, shift, scale, w, *, tm, tn, out_dtype, cos=None, sin=None,
               transpose_h=False, vmem_mib=56):
    m, k = x.shape
    n = w.shape[1]
    rope_last = cos is not None
    row = lambda a: a.reshape(1, k)
    in_specs = [pl.BlockSpec((tm, k), lambda i, j: (i, 0)),
                pl.BlockSpec((1, k), lambda i, j: (0, 0)),
                pl.BlockSpec((1, k), lambda i, j: (0, 0)),
                pl.BlockSpec((1, k), lambda i, j: (0, 0)),
                pl.BlockSpec((k, tn), lambda i, j: (0, j))]
    args = [x, row(g), row(shift), row(scale), w]
    if rope_last:
        assert tn == n
        in_specs += [pl.BlockSpec((tm, LANE), lambda i, j: (i, 0)),
                     pl.BlockSpec((tm, LANE), lambda i, j: (i, 0))]
        args += [cos, sin]
    if transpose_h:
        h_shape, h_spec = (m // tm, k, tm), pl.BlockSpec((1, k, tm), lambda i, j: (i, 0, 0))
    else:
        h_shape, h_spec = (m, k), pl.BlockSpec((tm, k), lambda i, j: (i, 0))
    return pl.pallas_call(
        functools.partial(_norm_proj_kernel, rope_last=rope_last, transpose_h=transpose_h,
                          row_chunk=min(tm, LANE)),
        grid=(m // tm, n // tn),
        in_specs=in_specs,
        out_specs=[pl.BlockSpec((tm, tn), lambda i, j: (i, j)), h_spec],
        out_shape=[jax.ShapeDtypeStruct((m, n), out_dtype),
                   jax.ShapeDtypeStruct(h_shape, BF16)],
        scratch_shapes=[pltpu.VMEM((tm, k), BF16)],
        compiler_params=_cparams(("arbitrary", "arbitrary"), vmem_mib),
        name="norm_proj",
    )(*args)


def _matmul_kernel(a_ref, w_ref, o_ref):
    o_ref[...] = jnp.dot(a_ref[...], w_ref[...], preferred_element_type=F32).astype(o_ref.dtype)


def _matmul(a, w, *, tm, tn, out_dtype, vmem_mib=48):
    m, k = a.shape
    n = w.shape[1]
    return pl.pallas_call(
        _matmul_kernel,
        grid=(m // tm, n // tn),
        in_specs=[pl.BlockSpec((tm, k), lambda i, j: (i, 0)),
                  pl.BlockSpec((k, tn), lambda i, j: (0, j))],
        out_specs=pl.BlockSpec((tm, tn), lambda i, j: (i, j)),
        out_shape=jax.ShapeDtypeStruct((m, n), out_dtype),
        compiler_params=_cparams(("arbitrary", "arbitrary"), vmem_mib),
        name="matmul",
    )(a, w)


def _conv_kernel(xin_ref, gb_ref, gc_ref, w_ref, b_ref, o_ref):
    t = xin_ref.shape[0]
    u = gc_ref[...] * xin_ref[...]
    row = lax.broadcasted_iota(jnp.int32, u.shape, 0)
    prev = jnp.where(row == 0, 0.0, pltpu.roll(u, 1, axis=0))
    nxt = jnp.where(row == t - 1, 0.0, pltpu.roll(u, t - 1, axis=0))
    y = prev * w_ref[0:1, :] + u * w_ref[1:2, :] + nxt * w_ref[2:3, :] + b_ref[...]
    o_ref[...] = (gb_ref[...] * y).astype(o_ref.dtype)


def _short_conv(pconv, w, b):
    t = pconv.shape[0]
    nb = CONV_CH // LANE
    return pl.pallas_call(
        _conv_kernel,
        grid=(nb,),
        in_specs=[pl.BlockSpec((t, LANE), lambda j: (0, j)),
                  pl.BlockSpec((t, LANE), lambda j: (0, nb + j)),
                  pl.BlockSpec((t, LANE), lambda j: (0, 2 * nb + j)),
                  pl.BlockSpec((3, LANE), lambda j: (0, j)),
                  pl.BlockSpec((1, LANE), lambda j: (0, j))],
        out_specs=pl.BlockSpec((t, LANE), lambda j: (0, j)),
        out_shape=jax.ShapeDtypeStruct((t, CONV_CH), BF16),
        compiler_params=_cparams(("arbitrary",), 56),
        name="short_conv",
    )(pconv, pconv, pconv, w, b.reshape(1, CONV_CH))


def _q_proj_kernel(cq_ref, g_ref, w_ref, cos_ref, sin_ref, q_ref):
    y = _rms(cq_ref[...], g_ref[...]).astype(BF16)
    acc = jnp.dot(y, w_ref[...], preferred_element_type=F32)
    cos = cos_ref[...]
    sin = sin_ref[...]
    for h in range(MLA_HEADS):
        lo = h * HEAD_PAD
        q_ref[h, :, :LANE] = (acc[:, lo:lo + LANE] * Q_SCALE).astype(BF16)
        xr = acc[:, lo + LANE:lo + HEAD_PAD]
        rope = xr * cos + pltpu.roll(xr, LANE // 2, axis=1) * sin
        q_ref[h, :, LANE:] = (rope * Q_SCALE).astype(BF16)


def _q_proj(pa, g, w, cos, sin, *, tm):
    m = pa.shape[0]
    n = w.shape[1]
    return pl.pallas_call(
        _q_proj_kernel,
        grid=(m // tm,),
        in_specs=[pl.BlockSpec((tm, Q_LORA), lambda i: (i, 0)),
                  pl.BlockSpec((1, Q_LORA), lambda i: (0, 0)),
                  pl.BlockSpec((Q_LORA, n), lambda i: (0, 0)),
                  pl.BlockSpec((tm, LANE), lambda i: (i, 0)),
                  pl.BlockSpec((tm, LANE), lambda i: (i, 0))],
        out_specs=pl.BlockSpec((MLA_HEADS, tm, HEAD_PAD), lambda i: (0, i, 0)),
        out_shape=jax.ShapeDtypeStruct((MLA_HEADS, m, HEAD_PAD), BF16),
        compiler_params=_cparams(("arbitrary",), 48),
        name="q_proj",
    )(pa, g.reshape(1, Q_LORA), w, cos, sin)


def _kv_proj_kernel(ckv_ref, g_ref, kr_ref, wk_ref, wvt_ref, k_ref, vt_ref):
    y = _rms(ckv_ref[...], g_ref[...]).astype(BF16)
    kn = jnp.dot(y, wk_ref[...], preferred_element_type=F32)
    vt_ref[...] = lax.dot_general(wvt_ref[...], y, (((1,), (1,)), ((), ())),
                                  preferred_element_type=F32).astype(BF16)
    kr = kr_ref[...].astype(BF16)
    for h in range(MLA_HEADS):
        k_ref[h, :, :LANE] = kn[:, h * QK_NOPE:(h + 1) * QK_NOPE].astype(BF16)
        k_ref[h, :, LANE:] = kr


def _kv_proj(ckv, g, krot, wk, wvt, *, tm):
    m = ckv.shape[0]
    nv = MLA_HEADS * V_HEAD
    return pl.pallas_call(
        _kv_proj_kernel,
        grid=(m // tm,),
        in_specs=[pl.BlockSpec((tm, KV_LORA), lambda i: (i, 0)),
                  pl.BlockSpec((1, KV_LORA), lambda i: (0, 0)),
                  pl.BlockSpec((tm, LANE), lambda i: (i, 0)),
                  pl.BlockSpec(wk.shape, lambda i: (0, 0)),
                  pl.BlockSpec(wvt.shape, lambda i: (0, 0))],
        out_specs=[pl.BlockSpec((MLA_HEADS, tm, HEAD_PAD), lambda i: (0, i, 0)),
                   pl.BlockSpec((nv, tm), lambda i: (0, i))],
        out_shape=[jax.ShapeDtypeStruct((MLA_HEADS, m, HEAD_PAD), BF16),
                   jax.ShapeDtypeStruct((nv, m), BF16)],
        compiler_params=_cparams(("arbitrary",), 48),
        name="kv_proj",
    )(ckv, g.reshape(1, KV_LORA), krot, wk, wvt)


def _attn_kernel(q_ref, k_ref, vt_ref, o_ref):
    chunk = KV_CHUNK
    n_chunks = vt_ref.shape[1] // chunk
    q = q_ref[0]
    tq = q.shape[0]
    nt = (((1,), (1,)), ((), ()))
    m = jnp.full((1, tq), NEG, F32)
    l = jnp.zeros((1, tq), F32)
    acc = jnp.zeros((V_HEAD, tq), F32)

    def scores(c):
        return lax.dot_general(k_ref[0, c * chunk:(c + 1) * chunk, :], q, nt,
                               preferred_element_type=F32)

    pending = [scores(c) for c in range(min(QK_AHEAD, n_chunks))]
    for c in range(n_chunks):
        s = pending.pop(0)
        if c + QK_AHEAD < n_chunks:
            pending.append(scores(c + QK_AHEAD))
        m_new = jnp.maximum(m, s.max(axis=0, keepdims=True))
        p = jnp.exp2(s - m_new)
        alpha = jnp.exp2(m - m_new)
        l = alpha * l + p.sum(axis=0, keepdims=True)
        acc = alpha * acc + jnp.dot(vt_ref[:, c * chunk:(c + 1) * chunk], p.astype(BF16),
                                    preferred_element_type=F32)
        m = m_new
    o_ref[...] = (acc / l).T.astype(o_ref.dtype)


def _attention(q, k, vt, *, tq):
    s = q.shape[1]
    kv = k.shape[1]
    assert kv % KV_CHUNK == 0
    return pl.pallas_call(
        _attn_kernel,
        grid=(MLA_HEADS, s // tq),
        in_specs=[pl.BlockSpec((1, tq, HEAD_PAD), lambda h, i: (h, i, 0)),
                  pl.BlockSpec((1, kv, HEAD_PAD), lambda h, i: (h, 0, 0)),
                  pl.BlockSpec((V_HEAD, kv), lambda h, i: (h, 0))],
        out_specs=pl.BlockSpec((tq, V_HEAD), lambda h, i: (i, h)),
        out_shape=jax.ShapeDtypeStruct((s, MLA_HEADS * V_HEAD), BF16),
        compiler_params=_cparams(("arbitrary", "arbitrary"), 48),
        name="attention",
    )(q, k, vt)


def _out_proj_kernel(att_ref, conv_ref, wa_ref, wc_ref, x_ref, gate_ref, o_ref):
    acc = jnp.dot(att_ref[...], wa_ref[...], preferred_element_type=F32)
    acc = acc + jnp.dot(conv_ref[...], wc_ref[...], preferred_element_type=F32)
    o_ref[...] = x_ref[...] + gate_ref[...] * acc


def _out_proj(att, conv, wa, wc, x, gate, *, tm, tn):
    m, ka = att.shape
    kc = conv.shape[1]
    n = wa.shape[1]
    return pl.pallas_call(
        _out_proj_kernel,
        grid=(m // tm, n // tn),
        in_specs=[pl.BlockSpec((tm, ka), lambda i, j: (i, 0)),
                  pl.BlockSpec((tm, kc), lambda i, j: (i, 0)),
                  pl.BlockSpec((ka, tn), lambda i, j: (0, j)),
                  pl.BlockSpec((kc, tn), lambda i, j: (0, j)),
                  pl.BlockSpec((tm, tn), lambda i, j: (i, j)),
                  pl.BlockSpec((1, tn), lambda i, j: (0, j))],
        out_specs=pl.BlockSpec((tm, tn), lambda i, j: (i, j)),
        out_shape=jax.ShapeDtypeStruct((m, n), F32),
        compiler_params=_cparams(("arbitrary", "arbitrary"), 48),
        name="out_proj",
    )(att, conv, wa, wc, x, gate.reshape(1, n))


def _top16(pieces, lanes):
    rid = lax.broadcasted_iota(jnp.int32, (PEER_TOPK, lanes), 0)
    tops = jnp.zeros((PEER_TOPK, lanes), F32)
    for k in range(PEER_TOPK):
        m = pieces[0].max(axis=0, keepdims=True)
        for p in pieces[1:]:
            m = jnp.maximum(m, p.max(axis=0, keepdims=True))
        tops = jnp.where(rid == k, m, tops)
        if k + 1 < PEER_TOPK:
            pieces = [jnp.where(p == m, NEG, p) for p in pieces]
    return tops


def _peer_gate_kernel(q_ref, keys_ref, s1_ref, e1_ref, s2_ref, e2_ref, thr_ref):
    lanes = q_ref.shape[0]
    n_groups, na = s1_ref.shape[1], s1_ref.shape[3]
    nt = (((1,), (1,)), ((), ()))
    for h in range(PEER_HEADS):
        q1 = q_ref[:, (2 * h) * LANE:(2 * h + 1) * LANE]
        q2 = q_ref[:, (2 * h + 1) * LANE:(2 * h + 2) * LANE]
        s1 = lax.dot_general(keys_ref[2 * h], q1, nt, preferred_element_type=F32)
        s2 = lax.dot_general(keys_ref[2 * h + 1], q2, nt, preferred_element_type=F32)
        t1 = _top16([s1], lanes)
        t2 = _top16([s2], lanes)
        cands = [t1[0:1] + t2]
        cands += [t1[i:i + 1] + t2[0:8] for i in range(1, 8)]
        cands += [t1[8:16] + t2[0:1]]
        top = _top16(cands, lanes)
        z = jnp.sum(jnp.exp(top - top[0:1]), axis=0, keepdims=True)
        e1 = jnp.exp(s1 - t1[0:1]) / z
        for r in range(n_groups):
            s1_ref[0, r, h] = s1[r * na:(r + 1) * na, :]
            e1_ref[0, r, h] = e1[r * na:(r + 1) * na, :]
        s2_ref[0, h] = s2
        e2_ref[0, h] = jnp.exp(s2 - t2[0:1])
        thr_ref[0, h] = top[PEER_TOPK - 1:PEER_TOPK]


def _peer_gates(q, keys, *, tt, na):
    t = q.shape[0]
    nb = t // tt
    row_shape = (nb, N_KEYS // na, PEER_HEADS, na, tt)
    tab_shape = (nb, PEER_HEADS, N_KEYS, tt)
    thr_shape = (nb, PEER_HEADS, 1, tt)
    blk = lambda shape: pl.BlockSpec((1,) + shape[1:], lambda i: (i,) + (0,) * (len(shape) - 1))
    return pl.pallas_call(
        _peer_gate_kernel,
        grid=(nb,),
        in_specs=[pl.BlockSpec((tt, PEER_HEADS * PEER_DK), lambda i: (i, 0)),
                  pl.BlockSpec(keys.shape, lambda i: (0, 0, 0))],
        out_specs=[blk(row_shape), blk(row_shape), blk(tab_shape), blk(tab_shape),
                   blk(thr_shape)],
        out_shape=[jax.ShapeDtypeStruct(s, F32) for s in
                   (row_shape, row_shape, tab_shape, tab_shape, thr_shape)],
        compiler_params=_cparams(("arbitrary",), 48),
        name="peer_gates",
    )(q, keys)


def _peer_kernel(ht_ref, u_ref, vt_ref, s1_ref, e1_ref, s2_ref, e2_ref, thr_ref, o_ref,
                 act_a, act_b, w_a):
    j = pl.program_id(1)
    te, tm = w_a.shape

    @pl.when(j == 0)
    def _():
        o_ref[...] = jnp.zeros(o_ref.shape, F32)
        act_a[...] = jnp.zeros(act_a.shape, F32)

    def step(act_rd, act_wr):
        act_wr[...] = jnp.dot(u_ref[...], ht_ref[0], preferred_element_type=F32)
        for half in range(te // PEER_HALF):
            for a in range(half * (PEER_HALF // N_KEYS), (half + 1) * (PEER_HALF // N_KEYS)):
                rows = slice(a * N_KEYS, (a + 1) * N_KEYS)
                for c in range(tm // LANE):
                    cols = slice(c * LANE, (c + 1) * LANE)
                    act = act_rd[rows, cols]
                    gel = 0.5 * act * (1.0 + lax.erf(act * (1.0 / math.sqrt(2.0))))
                    g = jnp.zeros((N_KEYS, LANE), F32)
                    for h in range(PEER_HEADS):
                        s1row = s1_ref[0, 0, h, a:a + 1, cols]
                        e1row = e1_ref[0, 0, h, a:a + 1, cols]
                        sel = jnp.where(s1row + s2_ref[0, h, :, cols] >= thr_ref[0, h, :, cols],
                                        e2_ref[0, h, :, cols], 0.0)
                        g = g + e1row * sel
                    w_a[rows, cols] = (g * gel).astype(BF16)
            hrows = slice(half * PEER_HALF, (half + 1) * PEER_HALF)
            o_ref[0] += jnp.dot(vt_ref[0, :, hrows], w_a[hrows, :], preferred_element_type=F32)

    @pl.when(j % 2 == 0)
    def _():
        step(act_a, act_b)

    @pl.when(j % 2 == 1)
    def _():
        step(act_b, act_a)


def _peer_dense(ht, u, vt, tabs):
    nb, d, tm = ht.shape
    n_tiles, _, te = vt.shape
    s1, e1, s2, e2, thr = tabs
    na = te // N_KEYS
    assert s1.shape == (nb, N_KEYS // na, PEER_HEADS, na, tm)
    tile = lambda j, lag: jnp.clip(j - lag, 0, n_tiles - 1)
    row_spec = pl.BlockSpec((1, 1, PEER_HEADS, na, tm), lambda i, j: (i, tile(j, 1), 0, 0, 0))
    tab_spec = pl.BlockSpec((1, PEER_HEADS, N_KEYS, tm), lambda i, j: (i, 0, 0, 0))
    return pl.pallas_call(
        _peer_kernel,
        grid=(nb, n_tiles + 1),
        in_specs=[pl.BlockSpec((1, d, tm), lambda i, j: (i, 0, 0)),
                  pl.BlockSpec((te, d), lambda i, j: (tile(j, 0), 0)),
                  pl.BlockSpec((1, d, te), lambda i, j: (tile(j, 1), 0, 0)),
                  row_spec, row_spec, tab_spec, tab_spec,
                  pl.BlockSpec((1, PEER_HEADS, 1, tm), lambda i, j: (i, 0, 0, 0))],
        out_specs=pl.BlockSpec((1, d, tm), lambda i, j: (i, 0, 0)),
        out_shape=jax.ShapeDtypeStruct((nb, d, tm), F32),
        scratch_shapes=[pltpu.VMEM((te, tm), F32), pltpu.VMEM((te, tm), F32),
                        pltpu.VMEM((te, tm), BF16)],
        compiler_params=_cparams(("arbitrary", "arbitrary"), 60),
        name="peer_dense",
    )(ht, u, vt, s1, e1, s2, e2, thr)


def _final_kernel(pt_ref, x_ref, gate_ref, g_ref, o_ref):
    x2 = x_ref[...] + gate_ref[...] * pt_ref[0].T
    o_ref[...] = _rms(x2, g_ref[...])


def _final(pt, x1, gate, g, *, tf):
    nb, d, tm = pt.shape
    t = nb * tm
    per = tm // tf
    return pl.pallas_call(
        _final_kernel,
        grid=(t // tf,),
        in_specs=[pl.BlockSpec((1, d, tf), lambda i: (i // per, 0, i % per)),
                  pl.BlockSpec((tf, d), lambda i: (i, 0)),
                  pl.BlockSpec((1, d), lambda i: (0, 0)),
                  pl.BlockSpec((1, d), lambda i: (0, 0))],
        out_specs=pl.BlockSpec((tf, d), lambda i: (i, 0)),
        out_shape=jax.ShapeDtypeStruct((t, d), F32),
        compiler_params=_cparams(("arbitrary",), 48),
        name="final",
    )(pt, x1, gate.reshape(1, d), g.reshape(1, d))


def _rope_perm():
    j = np.arange(QK_ROPE)
    return np.where((j % 32) < 16, j + 16, j - 16)


def _rope_tables(t):
    f32 = np.float32
    rows = t // GRID_W
    row = np.repeat(np.arange(rows), GRID_W).astype(f32)
    col = np.tile(np.arange(GRID_W), rows).astype(f32)
    half = QK_ROPE // 2
    freqs = (f32(ROPE_THETA) ** (-np.arange(0, half, 2, dtype=f32) / f32(half))).astype(f32)
    ar = (row[:, None] * freqs).astype(np.float64)
    ac = (col[:, None] * freqs).astype(np.float64)
    zeros = np.zeros((t, LANE - QK_ROPE))
    cos = np.concatenate([np.cos(ar), np.cos(ar), np.cos(ac), np.cos(ac), zeros], axis=1)
    sin = np.concatenate([-np.sin(ar), np.sin(ar), -np.sin(ac), np.sin(ac), zeros], axis=1)
    return jnp.asarray(cos.astype(f32)), jnp.asarray(sin.astype(f32))


def kernel(x, c, ctx, c_ctx, w_ada, b_ada, norm1_g, w_in, q_norm_g, w_uq, kv_norm_g, w_ukv,
           conv_w, conv_b, w_o, norm2_g, peer_wq, peer_keys, peer_u, peer_v, final_norm_g):
    assert x.shape == (1, SEQ, D_MODEL) and ctx.shape == (1, CTX_LEN, D_MODEL)
    assert w_ada.shape[0] == 1
    x2 = x[0]
    ctx2 = ctx[0]
    perm = _rope_perm()

    w_in0 = w_in[0]
    w_kr_perm = w_in0[:, O_KR:O_CONV][:, perm]
    w_a_x = jnp.concatenate([w_in0[:, :O_CONV], w_kr_perm], axis=1).astype(BF16)
    w_a_c = jnp.concatenate([w_in0[:, O_CKV:O_CONV], w_kr_perm], axis=1).astype(BF16)
    w_conv = w_in0[:, O_CONV:].astype(BF16)
    wq = w_uq[0].reshape(Q_LORA, MLA_HEADS, QK_NOPE + QK_ROPE)
    wq = jnp.concatenate([wq, wq[:, :, QK_NOPE:][:, :, perm]], axis=2)
    wq = wq.reshape(Q_LORA, MLA_HEADS * HEAD_PAD).astype(BF16)
    wkv = w_ukv[0].reshape(KV_LORA, MLA_HEADS, QK_NOPE + V_HEAD)
    wk = wkv[:, :, :QK_NOPE].reshape(KV_LORA, MLA_HEADS * QK_NOPE).astype(BF16)
    wvt = wkv[:, :, QK_NOPE:].reshape(KV_LORA, MLA_HEADS * V_HEAD).T.astype(BF16)
    wo_att = w_o[0, :MLA_HEADS * V_HEAD].astype(BF16)
    wo_conv = w_o[0, MLA_HEADS * V_HEAD:].astype(BF16)
    w_pq = peer_wq[0].astype(BF16)
    keys = peer_keys[0].reshape(PEER_HEADS * 2, N_KEYS, PEER_DK // 2).astype(BF16)
    u_bf = peer_u[0].astype(BF16)
    vt_bf = peer_v[0].astype(BF16).reshape(N_EXPERTS // PEER_TE, PEER_TE, D_MODEL)
    vt_bf = vt_bf.transpose(0, 2, 1)

    cos_x, sin_x = _rope_tables(SEQ)
    cos_c = jnp.concatenate([jnp.ones((CTX_LEN, QK_ROPE), F32),
                             jnp.zeros((CTX_LEN, LANE - QK_ROPE), F32)], axis=1)
    sin_c = jnp.zeros((CTX_LEN, LANE), F32)

    cond8 = jnp.concatenate([c, c_ctx[None, :], jnp.zeros((6, D_MODEL), F32)], axis=0)
    mods = _mods(cond8, w_ada[0], b_ada[0])
    m_x = mods[0].reshape(N_MOD, D_MODEL)
    m_c = mods[1].reshape(N_MOD, D_MODEL)

    pa_x, hx = _norm_proj(x2, norm1_g[0], m_x[0], m_x[1], w_a_x, tm=256, tn=w_a_x.shape[1],
                          out_dtype=F32, cos=cos_x, sin=sin_x)
    pa_c, _ = _norm_proj(ctx2, norm1_g[0], m_c[0], m_c[1], w_a_c, tm=CTX_LEN, tn=w_a_c.shape[1],
                         out_dtype=F32, cos=cos_c, sin=sin_c)
    pconv = _matmul(hx, w_conv, tm=1024, tn=1024, out_dtype=F32)
    conv = _short_conv(pconv, conv_w[0], conv_b[0])

    q = _q_proj(pa_x, q_norm_g[0], wq, cos_x, sin_x, tm=256)
    ckv = jnp.concatenate([pa_x[:, O_CKV:O_KR], pa_c[:, :KV_LORA]], axis=0)
    krot = jnp.concatenate([pa_x[:, O_KR:], pa_c[:, KV_LORA:]], axis=0)
    k, vt = _kv_proj(ckv, kv_norm_g[0], krot, wk, wvt, tm=384)
    att = _attention(q, k, vt, tq=512)

    x1 = _out_proj(att, conv, wo_att, wo_conv, x2, m_x[2], tm=1024, tn=512)

    qp, h2t = _norm_proj(x1, norm2_g[0], m_x[3], m_x[4], w_pq, tm=512, tn=1024,
                         out_dtype=BF16, transpose_h=True)
    tabs = _peer_gates(qp, keys, tt=512, na=PEER_TE // N_KEYS)
    peer_t = _peer_dense(h2t, u_bf, vt_bf, tabs)
    out = _final(peer_t, x1, m_x[5], final_norm_g, tf=256)
    return out[None]
```

```python
import functools
import math

import numpy as np
import jax
import jax.numpy as jnp
from jax import lax
from jax.experimental import pallas as pl
from jax.experimental.pallas import tpu as pltpu

F32 = jnp.float32
BF16 = jnp.bfloat16

D_MODEL = 4096
SEQ = 8192
GRID_W = 64
CTX_LEN = 256
MLA_HEADS = 16
QK_NOPE = 128
QK_ROPE = 64
V_HEAD = 128
Q_LORA = 768
KV_LORA = 512
ROPE_THETA = 10000.0
SOFTMAX_SCALE = 1.0 / math.sqrt(QK_NOPE + QK_ROPE)
CONV_CH = 2048
PEER_HEADS = 8
N_KEYS = 128
N_EXPERTS = N_KEYS * N_KEYS
PEER_TOPK = 16
PEER_DK = 256
N_MOD = 6
EPS = 1e-6
O_CKV = Q_LORA
O_KR = Q_LORA + KV_LORA
O_CONV = O_KR + QK_ROPE

LANE = 128
BF16_SUBLANES = 16
HEAD_PAD = 256
KV_LEN = SEQ + CTX_LEN
KV_CHUNK = 1408
Q_SCALE = SOFTMAX_SCALE * math.log2(math.e)
QK_AHEAD = 2
PEER_TE = 512
PEER_HALF = 512
NEG = float(np.finfo(np.float32).min)
MIB = 1024 * 1024


def _cparams(sem, vmem_mib):
    return pltpu.CompilerParams(dimension_semantics=sem, vmem_limit_bytes=vmem_mib * MIB)


def _rms(x, g):
    ms = jnp.mean(x * x, axis=-1, keepdims=True)
    return x * lax.rsqrt(ms + EPS) * g


def _mods_kernel(cond_ref, w_ref, b_ref, o_ref):
    c = cond_ref[...]
    s = c * (1.0 / (1.0 + jnp.exp(-c)))
    o_ref[...] = jnp.dot(s.astype(BF16), w_ref[...].astype(BF16),
                         preferred_element_type=F32) + b_ref[...]


def _mods(cond8, w, b):
    n = w.shape[1]
    tn = 512
    return pl.pallas_call(
        _mods_kernel,
        grid=(n // tn,),
        in_specs=[pl.BlockSpec((8, D_MODEL), lambda j: (0, 0)),
                  pl.BlockSpec((D_MODEL, tn), lambda j: (0, j)),
                  pl.BlockSpec((1, tn), lambda j: (0, j))],
        out_specs=pl.BlockSpec((8, tn), lambda j: (0, j)),
        out_shape=jax.ShapeDtypeStruct((8, n), F32),
        compiler_params=_cparams(("arbitrary",), 48),
        name="mods",
    )(cond8, w, b.reshape(1, n))


def _norm_proj_kernel(*refs, rope_last, transpose_h, row_chunk):
    if rope_last:
        x_ref, g_ref, sh_ref, sc_ref, w_ref, cos_ref, sin_ref, o_ref, h_ref, h_scr = refs
    else:
        x_ref, g_ref, sh_ref, sc_ref, w_ref, o_ref, h_ref, h_scr = refs
    tm = x_ref.shape[0]

    @pl.when(pl.program_id(1) == 0)
    def _():
        for r in range(tm // row_chunk):
            rows = slice(r * row_chunk, (r + 1) * row_chunk)
            y = _rms(x_ref[rows, :], g_ref[...])
            h = y * (1.0 + sc_ref[...]) + sh_ref[...]
            hb = h.astype(BF16)
            h_scr[rows, :] = hb
            if transpose_h:
                h_ref[0, :, rows] = h.T.astype(BF16)
            else:
                h_ref[rows, :] = hb

    acc = jnp.dot(h_scr[...], w_ref[...], preferred_element_type=F32)
    if rope_last:
        n = acc.shape[1]
        o_ref[:, : n - LANE] = acc[:, : n - LANE].astype(o_ref.dtype)
        xr = acc[:, n - LANE:]
        o_ref[:, n - LANE:] = (xr * cos_ref[...]
                               + pltpu.roll(xr, LANE // 2, axis=1) * sin_ref[...]).astype(o_ref.dtype)
    else:
        o_ref[...] = acc.astype(o_ref.dtype)


def _norm_proj(x, g, shift, scale, w, *, tm, tn, out_dtype, cos=None, sin=None,
               transpose_h=False, vmem_mib=56):
    m, k = x.shape
    n = w.shape[1]
    rope_last = cos is not None
    row = lambda a: a.reshape(1, k)
    in_specs = [pl.BlockSpec((tm, k), lambda i, j: (i, 0)),
                pl.BlockSpec((1, k), lambda i, j: (0, 0)),
                pl.BlockSpec((1, k), lambda i, j: (0, 0)),
                pl.BlockSpec((1, k), lambda i, j: (0, 0)),
                pl.BlockSpec((k, tn), lambda i, j: (0, j))]
    args = [x, row(g), row(shift), row(scale), w]
    if rope_last:
        assert tn == n
        in_specs += [pl.BlockSpec((tm, LANE), lambda i, j: (i, 0)),
                     pl.BlockSpec((tm, LANE), lambda i, j: (i, 0))]
        args += [cos, sin]
    if transpose_h:
        h_shape, h_spec = (m // tm, k, tm), pl.BlockSpec((1, k, tm), lambda i, j: (i, 0, 0))
    else:
        h_shape, h_spec = (m, k), pl.BlockSpec((tm, k), lambda i, j: (i, 0))
    return pl.pallas_call(
        functools.partial(_norm_proj_kernel, rope_last=rope_last, transpose_h=transpose_h,
                          row_chunk=min(tm, LANE)),
        grid=(m // tm, n // tn),
        in_specs=in_specs,
        out_specs=[pl.BlockSpec((tm, tn), lambda i, j: (i, j)), h_spec],
        out_shape=[jax.ShapeDtypeStruct((m, n), out_dtype),
                   jax.ShapeDtypeStruct(h_shape, BF16)],
        scratch_shapes=[pltpu.VMEM((tm, k), BF16)],
        compiler_params=_cparams(("arbitrary", "arbitrary"), vmem_mib),
        name="norm_proj",
    )(*args)


def _matmul_kernel(a_ref, w_ref, o_ref):
    o_ref[...] = jnp.dot(a_ref[...], w_ref[...], preferred_element_type=F32).astype(o_ref.dtype)


def _matmul(a, w, *, tm, tn, out_dtype, vmem_mib=48):
    m, k = a.shape
    n = w.shape[1]
    return pl.pallas_call(
        _matmul_kernel,
        grid=(m // tm, n // tn),
        in_specs=[pl.BlockSpec((tm, k), lambda i, j: (i, 0)),
                  pl.BlockSpec((k, tn), lambda i, j: (0, j))],
        out_specs=pl.BlockSpec((tm, tn), lambda i, j: (i, j)),
        out_shape=jax.ShapeDtypeStruct((m, n), out_dtype),
        compiler_params=_cparams(("arbitrary", "arbitrary"), vmem_mib),
        name="matmul",
    )(a, w)


def _conv_kernel(xin_ref, gb_ref, gc_ref, w_ref, b_ref, o_ref):
    t = xin_ref.shape[0]
    u = gc_ref[...] * xin_ref[...]
    row = lax.broadcasted_iota(jnp.int32, u.shape, 0)
    prev = jnp.where(row == 0, 0.0, pltpu.roll(u, 1, axis=0))
    nxt = jnp.where(row == t - 1, 0.0, pltpu.roll(u, t - 1, axis=0))
    y = prev * w_ref[0:1, :] + u * w_ref[1:2, :] + nxt * w_ref[2:3, :] + b_ref[...]
    o_ref[...] = (gb_ref[...] * y).astype(o_ref.dtype)


def _short_conv(pconv, w, b):
    t = pconv.shape[0]
    nb = CONV_CH // LANE
    return pl.pallas_call(
        _conv_kernel,
        grid=(nb,),
        in_specs=[pl.BlockSpec((t, LANE), lambda j: (0, j)),
                  pl.BlockSpec((t, LANE), lambda j: (0, nb + j)),
                  pl.BlockSpec((t, LANE), lambda j: (0, 2 * nb + j)),
                  pl.BlockSpec((3, LANE), lambda j: (0, j)),
                  pl.BlockSpec((1, LANE), lambda j: (0, j))],
        out_specs=pl.BlockSpec((t, LANE), lambda j: (0, j)),
        out_shape=jax.ShapeDtypeStruct((t, CONV_CH), BF16),
        compiler_params=_cparams(("arbitrary",), 56),
        name="short_conv",
    )(pconv, pconv, pconv, w, b.reshape(1, CONV_CH))


def _q_proj_kernel(cq_ref, g_ref, w_ref, cos_ref, sin_ref, q_ref):
    y = _rms(cq_ref[...], g_ref[...]).astype(BF16)
    acc = jnp.dot(y, w_ref[...], preferred_element_type=F32)
    cos = cos_ref[...]
    sin = sin_ref[...]
    for h in range(MLA_HEADS):
        lo = h * HEAD_PAD
        q_ref[h, :, :LANE] = (acc[:, lo:lo + LANE] * Q_SCALE).astype(BF16)
        xr = acc[:, lo + LANE:lo + HEAD_PAD]
        rope = xr * cos + pltpu.roll(xr, LANE // 2, axis=1) * sin
        q_ref[h, :, LANE:] = (rope * Q_SCALE).astype(BF16)


def _q_proj(pa, g, w, cos, sin, *, tm):
    m = pa.shape[0]
    n = w.shape[1]
    return pl.pallas_call(
        _q_proj_kernel,
        grid=(m // tm,),
        in_specs=[pl.BlockSpec((tm, Q_LORA), lambda i: (i, 0)),
                  pl.BlockSpec((1, Q_LORA), lambda i: (0, 0)),
                  pl.BlockSpec((Q_LORA, n), lambda i: (0, 0)),
                  pl.BlockSpec((tm, LANE), lambda i: (i, 0)),
                  pl.BlockSpec((tm, LANE), lambda i: (i, 0))],
        out_specs=pl.BlockSpec((MLA_HEADS, tm, HEAD_PAD), lambda i: (0, i, 0)),
        out_shape=jax.ShapeDtypeStruct((MLA_HEADS, m, HEAD_PAD), BF16),
        compiler_params=_cparams(("arbitrary",), 48),
        name="q_proj",
    )(pa, g.reshape(1, Q_LORA), w, cos, sin)


def _kv_proj_kernel(ckv_ref, g_ref, kr_ref, wk_ref, wvt_ref, k_ref, vt_ref):
    y = _rms(ckv_ref[...], g_ref[...]).astype(BF16)
    kn = jnp.dot(y, wk_ref[...], preferred_element_type=F32)
    vt_ref[...] = lax.dot_general(wvt_ref[...], y, (((1,), (1,)), ((), ())),
                                  preferred_element_type=F32).astype(BF16)
    kr = kr_ref[...].astype(BF16)
    for h in range(MLA_HEADS):
        k_ref[h, :, :LANE] = kn[:, h * QK_NOPE:(h + 1) * QK_NOPE].astype(BF16)
        k_ref[h, :, LANE:] = kr


def _kv_proj(ckv, g, krot, wk, wvt, *, tm):
    m = ckv.shape[0]
    nv = MLA_HEADS * V_HEAD
    return pl.pallas_call(
        _kv_proj_kernel,
        grid=(m // tm,),
        in_specs=[pl.BlockSpec((tm, KV_LORA), lambda i: (i, 0)),
                  pl.BlockSpec((1, KV_LORA), lambda i: (0, 0)),
                  pl.BlockSpec((tm, LANE), lambda i: (i, 0)),
                  pl.BlockSpec(wk.shape, lambda i: (0, 0)),
                  pl.BlockSpec(wvt.shape, lambda i: (0, 0))],
        out_specs=[pl.BlockSpec((MLA_HEADS, tm, HEAD_PAD), lambda i: (0, i, 0)),
                   pl.BlockSpec((nv, tm), lambda i: (0, i))],
        out_shape=[jax.ShapeDtypeStruct((MLA_HEADS, m, HEAD_PAD), BF16),
                   jax.ShapeDtypeStruct((nv, m), BF16)],
        compiler_params=_cparams(("arbitrary",), 48),
        name="kv_proj",
    )(ckv, g.reshape(1, KV_LORA), krot, wk, wvt)


def _attn_kernel(q_ref, k_ref, vt_ref, o_ref):
    chunk = KV_CHUNK
    n_chunks = vt_ref.shape[1] // chunk
    q = q_ref[0]
    tq = q.shape[0]
    nt = (((1,), (1,)), ((), ()))
    m = jnp.full((1, tq), NEG, F32)
    l = jnp.zeros((1, tq), F32)
    acc = jnp.zeros((V_HEAD, tq), F32)

    def scores(c):
        return lax.dot_general(k_ref[0, c * chunk:(c + 1) * chunk, :], q, nt,
                               preferred_element_type=F32)

    pending = [scores(c) for c in range(min(QK_AHEAD, n_chunks))]
    for c in range(n_chunks):
        s = pending.pop(0)
        if c + QK_AHEAD < n_chunks:
            pending.append(scores(c + QK_AHEAD))
        m_new = jnp.maximum(m, s.max(axis=0, keepdims=True))
        p = jnp.exp2(s - m_new)
        alpha = jnp.exp2(m - m_new)
        l = alpha * l + p.sum(axis=0, keepdims=True)
        acc = alpha * acc + jnp.dot(vt_ref[:, c * chunk:(c + 1) * chunk], p.astype(BF16),
                                    preferred_element_type=F32)
        m = m_new
    o_ref[...] = (acc / l).T.astype(o_ref.dtype)


def _attention(q, k, vt, *, tq):
    s = q.shape[1]
    kv = k.shape[1]
    assert kv % KV_CHUNK == 0
    return pl.pallas_call(
        _attn_kernel,
        grid=(MLA_HEADS, s // tq),
        in_specs=[pl.BlockSpec((1, tq, HEAD_PAD), lambda h, i: (h, i, 0)),
                  pl.BlockSpec((1, kv, HEAD_PAD), lambda h, i: (h, 0, 0)),
                  pl.BlockSpec((V_HEAD, kv), lambda h, i: (h, 0))],
        out_specs=pl.BlockSpec((tq, V_HEAD), lambda h, i: (i, h)),
        out_shape=jax.ShapeDtypeStruct((s, MLA_HEADS * V_HEAD), BF16),
        compiler_params=_cparams(("arbitrary", "arbitrary"), 48),
        name="attention",
    )(q, k, vt)


def _out_proj_kernel(att_ref, conv_ref, wa_ref, wc_ref, x_ref, gate_ref, o_ref):
    acc = jnp.dot(att_ref[...], wa_ref[...].astype(BF16), preferred_element_type=F32)
    acc = acc + jnp.dot(conv_ref[...], wc_ref[...].astype(BF16), preferred_element_type=F32)
    o_ref[...] = x_ref[...] + gate_ref[...] * acc


def _out_proj(att, conv, w, x, gate, *, tm, tn):
    m, ka = att.shape
    kc = conv.shape[1]
    assert ka == kc and w.shape[0] == ka + kc
    n = w.shape[1]
    return pl.pallas_call(
        _out_proj_kernel,
        grid=(m // tm, n // tn),
        in_specs=[pl.BlockSpec((tm, ka), lambda i, j: (i, 0)),
                  pl.BlockSpec((tm, kc), lambda i, j: (i, 0)),
                  pl.BlockSpec((ka, tn), lambda i, j: (0, j)),
                  pl.BlockSpec((kc, tn), lambda i, j: (1, j)),
                  pl.BlockSpec((tm, tn), lambda i, j: (i, j)),
                  pl.BlockSpec((1, tn), lambda i, j: (0, j))],
        out_specs=pl.BlockSpec((tm, tn), lambda i, j: (i, j)),
        out_shape=jax.ShapeDtypeStruct((m, n), F32),
        compiler_params=_cparams(("arbitrary", "arbitrary"), 56),
        name="out_proj",
    )(att, conv, w, w, x, gate.reshape(1, n))


def _top16(pieces, lanes, want_rank=False):
    rid = lax.broadcasted_iota(jnp.int32, (PEER_TOPK, lanes), 0)
    tops = jnp.zeros((PEER_TOPK, lanes), F32)
    ranks = [jnp.full(p.shape, float(PEER_TOPK), F32) for p in pieces]
    for k in range(PEER_TOPK):
        m = pieces[0].max(axis=0, keepdims=True)
        for p in pieces[1:]:
            m = jnp.maximum(m, p.max(axis=0, keepdims=True))
        tops = jnp.where(rid == k, m, tops)
        if want_rank or k + 1 < PEER_TOPK:
            hit = [p == m for p in pieces]
            if want_rank:
                ranks = [jnp.where(e, float(k), r) for e, r in zip(hit, ranks)]
            pieces = [jnp.where(e, NEG, p) for e, p in zip(hit, pieces)]
    return (tops, ranks) if want_rank else tops


def _peer_gate_kernel(q_ref, keys_ref, c1_ref, e1_ref, r2_ref, e2_ref):
    lanes = q_ref.shape[0]
    n_groups, na = c1_ref.shape[1], c1_ref.shape[3]
    nt = (((1,), (1,)), ((), ()))
    rid = lax.broadcasted_iota(jnp.int32, (PEER_TOPK, lanes), 0)
    for h in range(PEER_HEADS):
        q1 = q_ref[:, (2 * h) * LANE:(2 * h + 1) * LANE]
        q2 = q_ref[:, (2 * h + 1) * LANE:(2 * h + 2) * LANE]
        s1 = lax.dot_general(keys_ref[2 * h], q1, nt, preferred_element_type=F32)
        s2 = lax.dot_general(keys_ref[2 * h + 1], q2, nt, preferred_element_type=F32)
        t1 = _top16([s1], lanes)
        t2, (rank2,) = _top16([s2], lanes, want_rank=True)
        cands = [t1[0:1] + t2]
        cands += [t1[i:i + 1] + t2[0:8] for i in range(1, 8)]
        cands += [t1[8:16] + t2[0:1]]
        top = _top16(cands, lanes)
        thr = top[PEER_TOPK - 1:PEER_TOPK]
        z = jnp.sum(jnp.exp(top - top[0:1]), axis=0, keepdims=True)
        count1 = jnp.zeros(s1.shape, F32)
        for i in range(PEER_TOPK):
            n_pass = jnp.sum(jnp.where(t1[i:i + 1] + t2 >= thr, 1.0, 0.0), axis=0, keepdims=True)
            count1 = jnp.where(s1 == t1[i:i + 1], n_pass, count1)
        e1 = jnp.exp(s1 - t1[0:1]) / z
        for r in range(n_groups):
            c1_ref[0, r, h] = count1[r * na:(r + 1) * na, :]
            e1_ref[0, r, h] = e1[r * na:(r + 1) * na, :]
        r2_ref[0, h] = rank2.astype(BF16)
        e2_ref[0, h] = jnp.exp(s2 - t2[0:1]).astype(BF16)


def _peer_gates(q, keys, *, tt, na):
    t = q.shape[0]
    nb = t // tt
    row_shape = (nb, N_KEYS // na, PEER_HEADS, na, tt)
    tab_shape = (nb, PEER_HEADS, N_KEYS, tt)
    blk = lambda shape: pl.BlockSpec((1,) + shape[1:], lambda i: (i,) + (0,) * (len(shape) - 1))
    return pl.pallas_call(
        _peer_gate_kernel,
        grid=(nb,),
        in_specs=[pl.BlockSpec((tt, PEER_HEADS * PEER_DK), lambda i: (i, 0)),
                  pl.BlockSpec(keys.shape, lambda i: (0, 0, 0))],
        out_specs=[blk(row_shape), blk(row_shape), blk(tab_shape), blk(tab_shape)],
        out_shape=[jax.ShapeDtypeStruct(row_shape, F32), jax.ShapeDtypeStruct(row_shape, F32),
                   jax.ShapeDtypeStruct(tab_shape, BF16), jax.ShapeDtypeStruct(tab_shape, BF16)],
        compiler_params=_cparams(("arbitrary",), 48),
        name="peer_gates",
    )(q, keys)


def _slab_kernel(v_ref, o_ref):
    o_ref[0] = v_ref[...].T.astype(BF16)


def _expert_slabs(v, *, te):
    ne, d = v.shape
    return pl.pallas_call(
        _slab_kernel,
        grid=(ne // te,),
        in_specs=[pl.BlockSpec((te, d), lambda i: (i, 0))],
        out_specs=pl.BlockSpec((1, d, te), lambda i: (i, 0, 0)),
        out_shape=jax.ShapeDtypeStruct((ne // te, d, te), BF16),
        compiler_params=_cparams(("arbitrary",), 48),
        name="expert_slabs",
    )(v)


def _rows_bf16(row):
    tile = jnp.broadcast_to(row, (BF16_SUBLANES, LANE)).astype(BF16)
    return jnp.concatenate([tile] * (N_KEYS // BF16_SUBLANES), axis=0)


def _peer_kernel(ht_ref, u_ref, vt_ref, c1_ref, e1_ref, r2_ref, e2_ref, o_ref,
                 act_a, act_b, w_a):
    j = pl.program_id(1)
    te, tm = w_a.shape

    @pl.when(j == 0)
    def _():
        o_ref[...] = jnp.zeros(o_ref.shape, F32)
        act_a[...] = jnp.zeros(act_a.shape, F32)

    def step(act_rd, act_wr):
        act_wr[...] = jnp.dot(u_ref[...], ht_ref[0], preferred_element_type=F32)
        for half in range(te // PEER_HALF):
            for a in range(half * (PEER_HALF // N_KEYS), (half + 1) * (PEER_HALF // N_KEYS)):
                rows = slice(a * N_KEYS, (a + 1) * N_KEYS)
                for c in range(tm // LANE):
                    cols = slice(c * LANE, (c + 1) * LANE)
                    act = act_rd[rows, cols]
                    gel = 0.5 * act * (1.0 + lax.erf(act * (1.0 / math.sqrt(2.0))))
                    g = jnp.zeros((N_KEYS, LANE), BF16)
                    for h in range(PEER_HEADS):
                        count = _rows_bf16(c1_ref[0, 0, h, a:a + 1, cols])
                        e1row = _rows_bf16(e1_ref[0, 0, h, a:a + 1, cols])
                        sel = jnp.where(r2_ref[0, h, :, cols] < count, e2_ref[0, h, :, cols],
                                        jnp.zeros((N_KEYS, LANE), BF16))
                        g = g + e1row * sel
                    w_a[rows, cols] = g * gel.astype(BF16)
            hrows = slice(half * PEER_HALF, (half + 1) * PEER_HALF)
            o_ref[0] += jnp.dot(vt_ref[0, :, hrows], w_a[hrows, :], preferred_element_type=F32)

    @pl.when(j % 2 == 0)
    def _():
        step(act_a, act_b)

    @pl.when(j % 2 == 1)
    def _():
        step(act_b, act_a)


def _peer_dense(ht, u, vt, tabs):
    nb, d, tm = ht.shape
    n_tiles, _, te = vt.shape
    c1, e1, r2, e2 = tabs
    na = te // N_KEYS
    assert c1.shape == (nb, N_KEYS // na, PEER_HEADS, na, tm)
    tile = lambda j, lag: jnp.clip(j - lag, 0, n_tiles - 1)
    row_spec = pl.BlockSpec((1, 1, PEER_HEADS, na, tm), lambda i, j: (i, tile(j, 1), 0, 0, 0))
    tab_spec = pl.BlockSpec((1, PEER_HEADS, N_KEYS, tm), lambda i, j: (i, 0, 0, 0))
    return pl.pallas_call(
        _peer_kernel,
        grid=(nb, n_tiles + 1),
        in_specs=[pl.BlockSpec((1, d, tm), lambda i, j: (i, 0, 0)),
                  pl.BlockSpec((te, d), lambda i, j: (tile(j, 0), 0)),
                  pl.BlockSpec((1, d, te), lambda i, j: (tile(j, 1), 0, 0)),
                  row_spec, row_spec, tab_spec, tab_spec],
        out_specs=pl.BlockSpec((1, d, tm), lambda i, j: (i, 0, 0)),
        out_shape=jax.ShapeDtypeStruct((nb, d, tm), F32),
        scratch_shapes=[pltpu.VMEM((te, tm), F32), pltpu.VMEM((te, tm), F32),
                        pltpu.VMEM((te, tm), BF16)],
        compiler_params=_cparams(("arbitrary", "arbitrary"), 60),
        name="peer_dense",
    )(ht, u, vt, c1, e1, r2, e2)


def _final_kernel(pt_ref, x_ref, gate_ref, g_ref, o_ref):
    x2 = x_ref[...] + gate_ref[...] * pt_ref[0].T
    o_ref[...] = _rms(x2, g_ref[...])


def _final(pt, x1, gate, g, *, tf):
    nb, d, tm = pt.shape
    t = nb * tm
    per = tm // tf
    return pl.pallas_call(
        _final_kernel,
        grid=(t // tf,),
        in_specs=[pl.BlockSpec((1, d, tf), lambda i: (i // per, 0, i % per)),
                  pl.BlockSpec((tf, d), lambda i: (i, 0)),
                  pl.BlockSpec((1, d), lambda i: (0, 0)),
                  pl.BlockSpec((1, d), lambda i: (0, 0))],
        out_specs=pl.BlockSpec((tf, d), lambda i: (i, 0)),
        out_shape=jax.ShapeDtypeStruct((t, d), F32),
        compiler_params=_cparams(("arbitrary",), 48),
        name="final",
    )(pt, x1, gate.reshape(1, d), g.reshape(1, d))


def _rope_perm():
    j = np.arange(QK_ROPE)
    return np.where((j % 32) < 16, j + 16, j - 16)


def _rope_tables(t):
    f32 = np.float32
    rows = t // GRID_W
    row = np.repeat(np.arange(rows), GRID_W).astype(f32)
    col = np.tile(np.arange(GRID_W), rows).astype(f32)
    half = QK_ROPE // 2
    freqs = (f32(ROPE_THETA) ** (-np.arange(0, half, 2, dtype=f32) / f32(half))).astype(f32)
    ar = (row[:, None] * freqs).astype(np.float64)
    ac = (col[:, None] * freqs).astype(np.float64)
    zeros = np.zeros((t, LANE - QK_ROPE))
    cos = np.concatenate([np.cos(ar), np.cos(ar), np.cos(ac), np.cos(ac), zeros], axis=1)
    sin = np.concatenate([-np.sin(ar), np.sin(ar), -np.sin(ac), np.sin(ac), zeros], axis=1)
    return jnp.asarray(cos.astype(f32)), jnp.asarray(sin.astype(f32))


def kernel(x, c, ctx, c_ctx, w_ada, b_ada, norm1_g, w_in, q_norm_g, w_uq, kv_norm_g, w_ukv,
           conv_w, conv_b, w_o, norm2_g, peer_wq, peer_keys, peer_u, peer_v, final_norm_g):
    assert x.shape == (1, SEQ, D_MODEL) and ctx.shape == (1, CTX_LEN, D_MODEL)
    assert w_ada.shape[0] == 1
    x2 = x[0]
    ctx2 = ctx[0]
    perm = _rope_perm()

    w_in0 = w_in[0]
    w_kr_perm = w_in0[:, O_KR:O_CONV][:, perm]
    w_a_x = jnp.concatenate([w_in0[:, :O_CONV], w_kr_perm], axis=1).astype(BF16)
    w_a_c = jnp.concatenate([w_in0[:, O_CKV:O_CONV], w_kr_perm], axis=1).astype(BF16)
    w_conv = w_in0[:, O_CONV:].astype(BF16)
    wq = w_uq[0].reshape(Q_LORA, MLA_HEADS, QK_NOPE + QK_ROPE)
    wq = jnp.concatenate([wq, wq[:, :, QK_NOPE:][:, :, perm]], axis=2)
    wq = wq.reshape(Q_LORA, MLA_HEADS * HEAD_PAD).astype(BF16)
    wkv = w_ukv[0].reshape(KV_LORA, MLA_HEADS, QK_NOPE + V_HEAD)
    wk = wkv[:, :, :QK_NOPE].reshape(KV_LORA, MLA_HEADS * QK_NOPE).astype(BF16)
    wvt = wkv[:, :, QK_NOPE:].reshape(KV_LORA, MLA_HEADS * V_HEAD).T.astype(BF16)
    w_pq = peer_wq[0].astype(BF16)
    keys = peer_keys[0].reshape(PEER_HEADS * 2, N_KEYS, PEER_DK // 2).astype(BF16)
    u_bf = peer_u[0].astype(BF16)
    vt_bf = _expert_slabs(peer_v[0], te=PEER_TE)

    cos_x, sin_x = _rope_tables(SEQ)
    cos_c = jnp.concatenate([jnp.ones((CTX_LEN, QK_ROPE), F32),
                             jnp.zeros((CTX_LEN, LANE - QK_ROPE), F32)], axis=1)
    sin_c = jnp.zeros((CTX_LEN, LANE), F32)

    cond8 = jnp.concatenate([c, c_ctx[None, :], jnp.zeros((6, D_MODEL), F32)], axis=0)
    mods = _mods(cond8, w_ada[0], b_ada[0])
    m_x = mods[0].reshape(N_MOD, D_MODEL)
    m_c = mods[1].reshape(N_MOD, D_MODEL)

    pa_x, hx = _norm_proj(x2, norm1_g[0], m_x[0], m_x[1], w_a_x, tm=256, tn=w_a_x.shape[1],
                          out_dtype=F32, cos=cos_x, sin=sin_x)
    pa_c, _ = _norm_proj(ctx2, norm1_g[0], m_c[0], m_c[1], w_a_c, tm=CTX_LEN, tn=w_a_c.shape[1],
                         out_dtype=F32, cos=cos_c, sin=sin_c)
    pconv = _matmul(hx, w_conv, tm=1024, tn=1024, out_dtype=F32)
    conv = _short_conv(pconv, conv_w[0], conv_b[0])

    q = _q_proj(pa_x, q_norm_g[0], wq, cos_x, sin_x, tm=256)
    ckv = jnp.concatenate([pa_x[:, O_CKV:O_KR], pa_c[:, :KV_LORA]], axis=0)
    krot = jnp.concatenate([pa_x[:, O_KR:], pa_c[:, KV_LORA:]], axis=0)
    k, vt = _kv_proj(ckv, kv_norm_g[0], krot, wk, wvt, tm=384)
    att = _attention(q, k, vt, tq=512)

    x1 = _out_proj(att, conv, w_o[0], x2, m_x[2], tm=1024, tn=512)

    qp, h2t = _norm_proj(x1, norm2_g[0], m_x[3], m_x[4], w_pq, tm=512, tn=1024,
                         out_dtype=BF16, transpose_h=True)
    tabs = _peer_gates(qp, keys, tt=512, na=PEER_TE // N_KEYS)
    peer_t = _peer_dense(h2t, u_bf, vt_bf, tabs)
    out = _final(peer_t, x1, m_x[5], final_norm_g, tf=256)
    return out[None]
```

```python
import functools
import math

import numpy as np
import jax
import jax.numpy as jnp
from jax import lax
from jax.experimental import pallas as pl
from jax.experimental.pallas import tpu as pltpu

F32 = jnp.float32
BF16 = jnp.bfloat16

D_MODEL = 4096
SEQ = 8192
GRID_W = 64
CTX_LEN = 256
MLA_HEADS = 16
QK_NOPE = 128
QK_ROPE = 64
V_HEAD = 128
Q_LORA = 768
KV_LORA = 512
ROPE_THETA = 10000.0
SOFTMAX_SCALE = 1.0 / math.sqrt(QK_NOPE + QK_ROPE)
CONV_CH = 2048
PEER_HEADS = 8
N_KEYS = 128
N_EXPERTS = N_KEYS * N_KEYS
PEER_TOPK = 16
PEER_DK = 256
N_MOD = 6
EPS = 1e-6
O_CKV = Q_LORA
O_KR = Q_LORA + KV_LORA
O_CONV = O_KR + QK_ROPE

LANE = 128
BF16_SUBLANES = 16
HEAD_PAD = 256
KV_LEN = SEQ + CTX_LEN
KV_CHUNK = 1408
Q_SCALE = SOFTMAX_SCALE * math.log2(math.e)
QK_AHEAD = 2
PEER_TE = 512
PEER_HALF = 512
NEG = float(np.finfo(np.float32).min)
MIB = 1024 * 1024


def _cparams(sem, vmem_mib):
    return pltpu.CompilerParams(dimension_semantics=sem, vmem_limit_bytes=vmem_mib * MIB)


def _rms(x, g):
    ms = jnp.mean(x * x, axis=-1, keepdims=True)
    return x * lax.rsqrt(ms + EPS) * g


def _mods_kernel(cond_ref, w_ref, b_ref, o_ref):
    c = cond_ref[...]
    s = c * (1.0 / (1.0 + jnp.exp(-c)))
    o_ref[...] = jnp.dot(s.astype(BF16), w_ref[...].astype(BF16),
                         preferred_element_type=F32) + b_ref[...]


def _mods(cond8, w, b):
    n = w.shape[1]
    tn = 512
    return pl.pallas_call(
        _mods_kernel,
        grid=(n // tn,),
        in_specs=[pl.BlockSpec((8, D_MODEL), lambda j: (0, 0)),
                  pl.BlockSpec((D_MODEL, tn), lambda j: (0, j)),
                  pl.BlockSpec((1, tn), lambda j: (0, j))],
        out_specs=pl.BlockSpec((8, tn), lambda j: (0, j)),
        out_shape=jax.ShapeDtypeStruct((8, n), F32),
        compiler_params=_cparams(("arbitrary",), 48),
        name="mods",
    )(cond8, w, b.reshape(1, n))


def _norm_proj_kernel(*refs, rope_last, transpose_h, row_chunk):
    if rope_last:
        x_ref, g_ref, sh_ref, sc_ref, w_ref, cos_ref, sin_ref, o_ref, h_ref, h_scr = refs
    else:
        x_ref, g_ref, sh_ref, sc_ref, w_ref, o_ref, h_ref, h_scr = refs
    tm = x_ref.shape[0]

    @pl.when(pl.program_id(1) == 0)
    def _():
        for r in range(tm // row_chunk):
            rows = slice(r * row_chunk, (r + 1) * row_chunk)
            y = _rms(x_ref[rows, :], g_ref[...])
            h = y * (1.0 + sc_ref[...]) + sh_ref[...]
            hb = h.astype(BF16)
            h_scr[rows, :] = hb
            if transpose_h:
                h_ref[0, :, rows] = h.T.astype(BF16)
            else:
                h_ref[rows, :] = hb

    acc = jnp.dot(h_scr[...], w_ref[...], preferred_element_type=F32)
    if rope_last:
        n = acc.shape[1]
        o_ref[:, : n - LANE] = acc[:, : n - LANE].astype(o_ref.dtype)
        xr = acc[:, n - LANE:]
        o_ref[:, n - LANE:] = (xr * cos_ref[...]
                               + pltpu.roll(xr, LANE // 2, axis=1) * sin_ref[...]).astype(o_ref.dtype)
    else:
        o_ref[...] = acc.astype(o_ref.dtype)


def _norm_proj(x, g, shift, scale, w, *, tm, tn, out_dtype, cos=None, sin=None,
               transpose_h=False, vmem_mib=56):
    m, k = x.shape
    n = w.shape[1]
    rope_last = cos is not None
    row = lambda a: a.reshape(1, k)
    in_specs = [pl.BlockSpec((tm, k), lambda i, j: (i, 0)),
                pl.BlockSpec((1, k), lambda i, j: (0, 0)),
                pl.BlockSpec((1, k), lambda i, j: (0, 0)),
                pl.BlockSpec((1, k), lambda i, j: (0, 0)),
                pl.BlockSpec((k, tn), lambda i, j: (0, j))]
    args = [x, row(g), row(shift), row(scale), w]
    if rope_last:
        assert tn == n
        in_specs += [pl.BlockSpec((tm, LANE), lambda i, j: (i, 0)),
                     pl.BlockSpec((tm, LANE), lambda i, j: (i, 0))]
        args += [cos, sin]
    if transpose_h:
        h_shape, h_spec = (m // tm, k, tm), pl.BlockSpec((1, k, tm), lambda i, j: (i, 0, 0))
    else:
        h_shape, h_spec = (m, k), pl.BlockSpec((tm, k), lambda i, j: (i, 0))
    return pl.pallas_call(
        functools.partial(_norm_proj_kernel, rope_last=rope_last, transpose_h=transpose_h,
                          row_chunk=min(tm, LANE)),
        grid=(m // tm, n // tn),
        in_specs=in_specs,
        out_specs=[pl.BlockSpec((tm, tn), lambda i, j: (i, j)), h_spec],
        out_shape=[jax.ShapeDtypeStruct((m, n), out_dtype),
                   jax.ShapeDtypeStruct(h_shape, BF16)],
        scratch_shapes=[pltpu.VMEM((tm, k), BF16)],
        compiler_params=_cparams(("arbitrary", "arbitrary"), vmem_mib),
        name="norm_proj",
    )(*args)


def _matmul_kernel(a_ref, w_ref, o_ref):
    o_ref[...] = jnp.dot(a_ref[...], w_ref[...], preferred_element_type=F32).astype(o_ref.dtype)


def _matmul(a, w, *, tm, tn, out_dtype, vmem_mib=48):
    m, k = a.shape
    n = w.shape[1]
    return pl.pallas_call(
        _matmul_kernel,
        grid=(m // tm, n // tn),
        in_specs=[pl.BlockSpec((tm, k), lambda i, j: (i, 0)),
                  pl.BlockSpec((k, tn), lambda i, j: (0, j))],
        out_specs=pl.BlockSpec((tm, tn), lambda i, j: (i, j)),
        out_shape=jax.ShapeDtypeStruct((m, n), out_dtype),
        compiler_params=_cparams(("arbitrary", "arbitrary"), vmem_mib),
        name="matmul",
    )(a, w)


def _conv_kernel(xin_ref, gb_ref, gc_ref, w_ref, b_ref, o_ref):
    t = xin_ref.shape[0]
    u = gc_ref[...] * xin_ref[...]
    row = lax.broadcasted_iota(jnp.int32, u.shape, 0)
    prev = jnp.where(row == 0, 0.0, pltpu.roll(u, 1, axis=0))
    nxt = jnp.where(row == t - 1, 0.0, pltpu.roll(u, t - 1, axis=0))
    y = prev * w_ref[0:1, :] + u * w_ref[1:2, :] + nxt * w_ref[2:3, :] + b_ref[...]
    o_ref[...] = (gb_ref[...] * y).astype(o_ref.dtype)


def _short_conv(pconv, w, b):
    t = pconv.shape[0]
    nb = CONV_CH // LANE
    return pl.pallas_call(
        _conv_kernel,
        grid=(nb,),
        in_specs=[pl.BlockSpec((t, LANE), lambda j: (0, j)),
                  pl.BlockSpec((t, LANE), lambda j: (0, nb + j)),
                  pl.BlockSpec((t, LANE), lambda j: (0, 2 * nb + j)),
                  pl.BlockSpec((3, LANE), lambda j: (0, j)),
                  pl.BlockSpec((1, LANE), lambda j: (0, j))],
        out_specs=pl.BlockSpec((t, LANE), lambda j: (0, j)),
        out_shape=jax.ShapeDtypeStruct((t, CONV_CH), BF16),
        compiler_params=_cparams(("arbitrary",), 56),
        name="short_conv",
    )(pconv, pconv, pconv, w, b.reshape(1, CONV_CH))


def _q_proj_kernel(cq_ref, g_ref, w_ref, cos_ref, sin_ref, q_ref):
    y = _rms(cq_ref[...], g_ref[...]).astype(BF16)
    acc = jnp.dot(y, w_ref[...], preferred_element_type=F32)
    cos = cos_ref[...]
    sin = sin_ref[...]
    for h in range(MLA_HEADS):
        lo = h * HEAD_PAD
        q_ref[h, :, :LANE] = (acc[:, lo:lo + LANE] * Q_SCALE).astype(BF16)
        xr = acc[:, lo + LANE:lo + HEAD_PAD]
        rope = xr * cos + pltpu.roll(xr, LANE // 2, axis=1) * sin
        q_ref[h, :, LANE:] = (rope * Q_SCALE).astype(BF16)


def _q_proj(pa, g, w, cos, sin, *, tm):
    m = pa.shape[0]
    n = w.shape[1]
    return pl.pallas_call(
        _q_proj_kernel,
        grid=(m // tm,),
        in_specs=[pl.BlockSpec((tm, Q_LORA), lambda i: (i, 0)),
                  pl.BlockSpec((1, Q_LORA), lambda i: (0, 0)),
                  pl.BlockSpec((Q_LORA, n), lambda i: (0, 0)),
                  pl.BlockSpec((tm, LANE), lambda i: (i, 0)),
                  pl.BlockSpec((tm, LANE), lambda i: (i, 0))],
        out_specs=pl.BlockSpec((MLA_HEADS, tm, HEAD_PAD), lambda i: (0, i, 0)),
        out_shape=jax.ShapeDtypeStruct((MLA_HEADS, m, HEAD_PAD), BF16),
        compiler_params=_cparams(("arbitrary",), 48),
        name="q_proj",
    )(pa, g.reshape(1, Q_LORA), w, cos, sin)


def _kv_proj_kernel(ckv_ref, g_ref, kr_ref, wk_ref, wvt_ref, k_ref, vt_ref):
    y = _rms(ckv_ref[...], g_ref[...]).astype(BF16)
    kn = jnp.dot(y, wk_ref[...], preferred_element_type=F32)
    vt_ref[...] = lax.dot_general(wvt_ref[...], y, (((1,), (1,)), ((), ())),
                                  preferred_element_type=F32).astype(BF16)
    kr = kr_ref[...].astype(BF16)
    for h in range(MLA_HEADS):
        k_ref[h, :, :LANE] = kn[:, h * QK_NOPE:(h + 1) * QK_NOPE].astype(BF16)
        k_ref[h, :, LANE:] = kr


def _kv_proj(ckv, g, krot, wk, wvt, *, tm):
    m = ckv.shape[0]
    nv = MLA_HEADS * V_HEAD
    return pl.pallas_call(
        _kv_proj_kernel,
        grid=(m // tm,),
        in_specs=[pl.BlockSpec((tm, KV_LORA), lambda i: (i, 0)),
                  pl.BlockSpec((1, KV_LORA), lambda i: (0, 0)),
                  pl.BlockSpec((tm, LANE), lambda i: (i, 0)),
                  pl.BlockSpec(wk.shape, lambda i: (0, 0)),
                  pl.BlockSpec(wvt.shape, lambda i: (0, 0))],
        out_specs=[pl.BlockSpec((MLA_HEADS, tm, HEAD_PAD), lambda i: (0, i, 0)),
                   pl.BlockSpec((nv, tm), lambda i: (0, i))],
        out_shape=[jax.ShapeDtypeStruct((MLA_HEADS, m, HEAD_PAD), BF16),
                   jax.ShapeDtypeStruct((nv, m), BF16)],
        compiler_params=_cparams(("arbitrary",), 48),
        name="kv_proj",
    )(ckv, g.reshape(1, KV_LORA), krot, wk, wvt)


def _attn_kernel(q_ref, k_ref, vt_ref, o_ref):
    chunk = KV_CHUNK
    n_chunks = vt_ref.shape[1] // chunk
    q = q_ref[0]
    tq = q.shape[0]
    nt = (((1,), (1,)), ((), ()))
    m = jnp.full((1, tq), NEG, F32)
    l = jnp.zeros((1, tq), F32)
    acc = jnp.zeros((V_HEAD, tq), F32)

    def scores(c):
        return lax.dot_general(k_ref[0, c * chunk:(c + 1) * chunk, :], q, nt,
                               preferred_element_type=F32)

    pending = [scores(c) for c in range(min(QK_AHEAD, n_chunks))]
    for c in range(n_chunks):
        s = pending.pop(0)
        if c + QK_AHEAD < n_chunks:
            pending.append(scores(c + QK_AHEAD))
        m_new = jnp.maximum(m, s.max(axis=0, keepdims=True))
        p = jnp.exp2(s - m_new)
        alpha = jnp.exp2(m - m_new)
        l = alpha * l + p.sum(axis=0, keepdims=True)
        acc = alpha * acc + jnp.dot(vt_ref[:, c * chunk:(c + 1) * chunk], p.astype(BF16),
                                    preferred_element_type=F32)
        m = m_new
    o_ref[...] = (acc / l).T.astype(o_ref.dtype)


def _attention(q, k, vt, *, tq):
    s = q.shape[1]
    kv = k.shape[1]
    assert kv % KV_CHUNK == 0
    return pl.pallas_call(
        _attn_kernel,
        grid=(MLA_HEADS, s // tq),
        in_specs=[pl.BlockSpec((1, tq, HEAD_PAD), lambda h, i: (h, i, 0)),
                  pl.BlockSpec((1, kv, HEAD_PAD), lambda h, i: (h, 0, 0)),
                  pl.BlockSpec((V_HEAD, kv), lambda h, i: (h, 0))],
        out_specs=pl.BlockSpec((tq, V_HEAD), lambda h, i: (i, h)),
        out_shape=jax.ShapeDtypeStruct((s, MLA_HEADS * V_HEAD), BF16),
        compiler_params=_cparams(("arbitrary", "arbitrary"), 48),
        name="attention",
    )(q, k, vt)


def _out_proj_kernel(att_ref, conv_ref, wa_ref, wc_ref, x_ref, gate_ref, o_ref):
    acc = jnp.dot(att_ref[...], wa_ref[...].astype(BF16), preferred_element_type=F32)
    acc = acc + jnp.dot(conv_ref[...], wc_ref[...].astype(BF16), preferred_element_type=F32)
    o_ref[...] = x_ref[...] + gate_ref[...] * acc


def _out_proj(att, conv, w, x, gate, *, tm, tn):
    m, ka = att.shape
    kc = conv.shape[1]
    assert ka == kc and w.shape[0] == ka + kc
    n = w.shape[1]
    return pl.pallas_call(
        _out_proj_kernel,
        grid=(m // tm, n // tn),
        in_specs=[pl.BlockSpec((tm, ka), lambda i, j: (i, 0)),
                  pl.BlockSpec((tm, kc), lambda i, j: (i, 0)),
                  pl.BlockSpec((ka, tn), lambda i, j: (0, j)),
                  pl.BlockSpec((kc, tn), lambda i, j: (1, j)),
                  pl.BlockSpec((tm, tn), lambda i, j: (i, j)),
                  pl.BlockSpec((1, tn), lambda i, j: (0, j))],
        out_specs=pl.BlockSpec((tm, tn), lambda i, j: (i, j)),
        out_shape=jax.ShapeDtypeStruct((m, n), F32),
        compiler_params=_cparams(("arbitrary", "arbitrary"), 56),
        name="out_proj",
    )(att, conv, w, w, x, gate.reshape(1, n))


def _top16(pieces, lanes, want_rank=False):
    rid = lax.broadcasted_iota(jnp.int32, (PEER_TOPK, lanes), 0)
    tops = jnp.zeros((PEER_TOPK, lanes), F32)
    ranks = [jnp.full(p.shape, float(PEER_TOPK), F32) for p in pieces]
    for k in range(PEER_TOPK):
        m = pieces[0].max(axis=0, keepdims=True)
        for p in pieces[1:]:
            m = jnp.maximum(m, p.max(axis=0, keepdims=True))
        tops = jnp.where(rid == k, m, tops)
        if want_rank or k + 1 < PEER_TOPK:
            hit = [p == m for p in pieces]
            if want_rank:
                ranks = [jnp.where(e, float(k), r) for e, r in zip(hit, ranks)]
            pieces = [jnp.where(e, NEG, p) for e, p in zip(hit, pieces)]
    return (tops, ranks) if want_rank else tops


def _peer_gate_kernel(q_ref, keys_ref, c1_ref, e1_ref, r2_ref, e2_ref):
    lanes = q_ref.shape[0]
    n_groups, na = c1_ref.shape[1], c1_ref.shape[3]
    nt = (((1,), (1,)), ((), ()))
    rid = lax.broadcasted_iota(jnp.int32, (PEER_TOPK, lanes), 0)
    for h in range(PEER_HEADS):
        q1 = q_ref[:, (2 * h) * LANE:(2 * h + 1) * LANE]
        q2 = q_ref[:, (2 * h + 1) * LANE:(2 * h + 2) * LANE]
        s1 = lax.dot_general(keys_ref[2 * h], q1, nt, preferred_element_type=F32)
        s2 = lax.dot_general(keys_ref[2 * h + 1], q2, nt, preferred_element_type=F32)
        t1 = _top16([s1], lanes)
        t2, (rank2,) = _top16([s2], lanes, want_rank=True)
        cands = [t1[0:1] + t2]
        cands += [t1[i:i + 1] + t2[0:8] for i in range(1, 8)]
        cands += [t1[8:16] + t2[0:1]]
        top = _top16(cands, lanes)
        thr = top[PEER_TOPK - 1:PEER_TOPK]
        z = jnp.sum(jnp.exp(top - top[0:1]), axis=0, keepdims=True)
        count1 = jnp.zeros(s1.shape, F32)
        for i in range(PEER_TOPK):
            n_pass = jnp.sum(jnp.where(t1[i:i + 1] + t2 >= thr, 1.0, 0.0), axis=0, keepdims=True)
            count1 = jnp.where(s1 == t1[i:i + 1], n_pass, count1)
        e1 = jnp.exp(s1 - t1[0:1]) / z
        for r in range(n_groups):
            c1_ref[0, r, h] = count1[r * na:(r + 1) * na, :]
            e1_ref[0, r, h] = e1[r * na:(r + 1) * na, :]
        r2_ref[0, h] = rank2.astype(BF16)
        e2_ref[0, h] = jnp.exp(s2 - t2[0:1]).astype(BF16)


def _peer_gates(q, keys, *, tt, na):
    t = q.shape[0]
    nb = t // tt
    row_shape = (nb, N_KEYS // na, PEER_HEADS, na, tt)
    tab_shape = (nb, PEER_HEADS, N_KEYS, tt)
    blk = lambda shape: pl.BlockSpec((1,) + shape[1:], lambda i: (i,) + (0,) * (len(shape) - 1))
    return pl.pallas_call(
        _peer_gate_kernel,
        grid=(nb,),
        in_specs=[pl.BlockSpec((tt, PEER_HEADS * PEER_DK), lambda i: (i, 0)),
                  pl.BlockSpec(keys.shape, lambda i: (0, 0, 0))],
        out_specs=[blk(row_shape), blk(row_shape), blk(tab_shape), blk(tab_shape)],
        out_shape=[jax.ShapeDtypeStruct(row_shape, F32), jax.ShapeDtypeStruct(row_shape, F32),
                   jax.ShapeDtypeStruct(tab_shape, BF16), jax.ShapeDtypeStruct(tab_shape, BF16)],
        compiler_params=_cparams(("arbitrary",), 48),
        name="peer_gates",
    )(q, keys)


def _slab_kernel(v_ref, o_ref):
    o_ref[0] = v_ref[...].T.astype(BF16)


def _expert_slabs(v, *, te):
    ne, d = v.shape
    return pl.pallas_call(
        _slab_kernel,
        grid=(ne // te,),
        in_specs=[pl.BlockSpec((te, d), lambda i: (i, 0))],
        out_specs=pl.BlockSpec((1, d, te), lambda i: (i, 0, 0)),
        out_shape=jax.ShapeDtypeStruct((ne // te, d, te), BF16),
        compiler_params=_cparams(("arbitrary",), 48),
        name="expert_slabs",
    )(v)


def _row_tile_bf16(row):
    return jnp.broadcast_to(row, (BF16_SUBLANES, LANE)).astype(BF16)[None]


def _peer_kernel(ht_ref, u_ref, vt_ref, c1_ref, e1_ref, r2_ref, e2_ref, o_ref,
                 act_a, act_b, w_a, *, n_tiles):
    j = pl.program_id(0)
    te, tm = w_a.shape

    @pl.when(j == 0)
    def _():
        act_a[...] = jnp.zeros(act_a.shape, F32)

    @pl.when(jnp.maximum(j - 1, 0) % n_tiles == 0)
    def _():
        o_ref[...] = jnp.zeros(o_ref.shape, F32)

    def step(act_rd, act_wr):
        act_wr[...] = jnp.dot(u_ref[...], ht_ref[0], preferred_element_type=F32)
        for half in range(te // PEER_HALF):
            for a in range(half * (PEER_HALF // N_KEYS), (half + 1) * (PEER_HALF // N_KEYS)):
                rows = slice(a * N_KEYS, (a + 1) * N_KEYS)
                for c in range(tm // LANE):
                    cols = slice(c * LANE, (c + 1) * LANE)
                    act = act_rd[rows, cols]
                    gel = 0.5 * act * (1.0 + lax.erf(act * (1.0 / math.sqrt(2.0))))
                    gshape = (N_KEYS // BF16_SUBLANES, BF16_SUBLANES, LANE)
                    g = jnp.zeros(gshape, BF16)
                    for h in range(PEER_HEADS):
                        count = _row_tile_bf16(c1_ref[0, 0, h, a:a + 1, cols])
                        e1row = _row_tile_bf16(e1_ref[0, 0, h, a:a + 1, cols])
                        sel = jnp.where(r2_ref[0, h, :, cols].reshape(gshape) < count,
                                        e2_ref[0, h, :, cols].reshape(gshape),
                                        jnp.zeros(gshape, BF16))
                        g = g + e1row * sel
                    w_a[rows, cols] = g.reshape(N_KEYS, LANE) * gel.astype(BF16)
            hrows = slice(half * PEER_HALF, (half + 1) * PEER_HALF)
            o_ref[0] += jnp.dot(vt_ref[0, :, hrows], w_a[hrows, :], preferred_element_type=F32)

    @pl.when(j % 2 == 0)
    def _():
        step(act_a, act_b)

    @pl.when(j % 2 == 1)
    def _():
        step(act_b, act_a)


def _peer_dense(ht, u, vt, tabs):
    nb, d, tm = ht.shape
    n_tiles, _, te = vt.shape
    c1, e1, r2, e2 = tabs
    na = te // N_KEYS
    assert c1.shape == (nb, N_KEYS // na, PEER_HEADS, na, tm)
    n_pairs = nb * n_tiles
    cur = lambda g: jnp.minimum(g, n_pairs - 1)
    prv = lambda g: jnp.maximum(g - 1, 0)
    row_spec = pl.BlockSpec((1, 1, PEER_HEADS, na, tm),
                            lambda g: (prv(g) // n_tiles, prv(g) % n_tiles, 0, 0, 0))
    tab_spec = pl.BlockSpec((1, PEER_HEADS, N_KEYS, tm), lambda g: (prv(g) // n_tiles, 0, 0, 0))
    return pl.pallas_call(
        functools.partial(_peer_kernel, n_tiles=n_tiles),
        grid=(n_pairs + 1,),
        in_specs=[pl.BlockSpec((1, d, tm), lambda g: (cur(g) // n_tiles, 0, 0)),
                  pl.BlockSpec((te, d), lambda g: (cur(g) % n_tiles, 0)),
                  pl.BlockSpec((1, d, te), lambda g: (prv(g) % n_tiles, 0, 0)),
                  row_spec, row_spec, tab_spec, tab_spec],
        out_specs=pl.BlockSpec((1, d, tm), lambda g: (prv(g) // n_tiles, 0, 0)),
        out_shape=jax.ShapeDtypeStruct((nb, d, tm), F32),
        scratch_shapes=[pltpu.VMEM((te, tm), F32), pltpu.VMEM((te, tm), F32),
                        pltpu.VMEM((te, tm), BF16)],
        compiler_params=_cparams(("arbitrary",), 60),
        name="peer_dense",
    )(ht, u, vt, c1, e1, r2, e2)


def _final_kernel(pt_ref, x_ref, gate_ref, g_ref, o_ref):
    x2 = x_ref[...] + gate_ref[...] * pt_ref[0].T
    o_ref[...] = _rms(x2, g_ref[...])


def _final(pt, x1, gate, g, *, tf):
    nb, d, tm = pt.shape
    t = nb * tm
    per = tm // tf
    return pl.pallas_call(
        _final_kernel,
        grid=(t // tf,),
        in_specs=[pl.BlockSpec((1, d, tf), lambda i: (i // per, 0, i % per)),
                  pl.BlockSpec((tf, d), lambda i: (i, 0)),
                  pl.BlockSpec((1, d), lambda i: (0, 0)),
                  pl.BlockSpec((1, d), lambda i: (0, 0))],
        out_specs=pl.BlockSpec((tf, d), lambda i: (i, 0)),
        out_shape=jax.ShapeDtypeStruct((t, d), F32),
        compiler_params=_cparams(("arbitrary",), 48),
        name="final",
    )(pt, x1, gate.reshape(1, d), g.reshape(1, d))


def _rope_perm():
    j = np.arange(QK_ROPE)
    return np.where((j % 32) < 16, j + 16, j - 16)


def _rope_tables(t):
    f32 = np.float32
    rows = t // GRID_W
    row = np.repeat(np.arange(rows), GRID_W).astype(f32)
    col = np.tile(np.arange(GRID_W), rows).astype(f32)
    half = QK_ROPE // 2
    freqs = (f32(ROPE_THETA) ** (-np.arange(0, half, 2, dtype=f32) / f32(half))).astype(f32)
    ar = (row[:, None] * freqs).astype(np.float64)
    ac = (col[:, None] * freqs).astype(np.float64)
    zeros = np.zeros((t, LANE - QK_ROPE))
    cos = np.concatenate([np.cos(ar), np.cos(ar), np.cos(ac), np.cos(ac), zeros], axis=1)
    sin = np.concatenate([-np.sin(ar), np.sin(ar), -np.sin(ac), np.sin(ac), zeros], axis=1)
    return jnp.asarray(cos.astype(f32)), jnp.asarray(sin.astype(f32))


def kernel(x, c, ctx, c_ctx, w_ada, b_ada, norm1_g, w_in, q_norm_g, w_uq, kv_norm_g, w_ukv,
           conv_w, conv_b, w_o, norm2_g, peer_wq, peer_keys, peer_u, peer_v, final_norm_g):
    assert x.shape == (1, SEQ, D_MODEL) and ctx.shape == (1, CTX_LEN, D_MODEL)
    assert w_ada.shape[0] == 1
    x2 = x[0]
    ctx2 = ctx[0]
    perm = _rope_perm()

    w_in0 = w_in[0]
    w_kr_perm = w_in0[:, O_KR:O_CONV][:, perm]
    w_a_x = jnp.concatenate([w_in0[:, :O_CONV], w_kr_perm], axis=1).astype(BF16)
    w_a_c = jnp.concatenate([w_in0[:, O_CKV:O_CONV], w_kr_perm], axis=1).astype(BF16)
    w_conv = w_in0[:, O_CONV:].astype(BF16)
    wq = w_uq[0].reshape(Q_LORA, MLA_HEADS, QK_NOPE + QK_ROPE)
    wq = jnp.concatenate([wq, wq[:, :, QK_NOPE:][:, :, perm]], axis=2)
    wq = wq.reshape(Q_LORA, MLA_HEADS * HEAD_PAD).astype(BF16)
    wkv = w_ukv[0].reshape(KV_LORA, MLA_HEADS, QK_NOPE + V_HEAD)
    wk = wkv[:, :, :QK_NOPE].reshape(KV_LORA, MLA_HEADS * QK_NOPE).astype(BF16)
    wvt = wkv[:, :, QK_NOPE:].reshape(KV_LORA, MLA_HEADS * V_HEAD).T.astype(BF16)
    w_pq = peer_wq[0].astype(BF16)
    keys = peer_keys[0].reshape(PEER_HEADS * 2, N_KEYS, PEER_DK // 2).astype(BF16)
    u_bf = peer_u[0].astype(BF16)
    vt_bf = _expert_slabs(peer_v[0], te=PEER_TE)

    cos_x, sin_x = _rope_tables(SEQ)
    cos_c = jnp.concatenate([jnp.ones((CTX_LEN, QK_ROPE), F32),
                             jnp.zeros((CTX_LEN, LANE - QK_ROPE), F32)], axis=1)
    sin_c = jnp.zeros((CTX_LEN, LANE), F32)

    cond8 = jnp.concatenate([c, c_ctx[None, :], jnp.zeros((6, D_MODEL), F32)], axis=0)
    mods = _mods(cond8, w_ada[0], b_ada[0])
    m_x = mods[0].reshape(N_MOD, D_MODEL)
    m_c = mods[1].reshape(N_MOD, D_MODEL)

    pa_x, hx = _norm_proj(x2, norm1_g[0], m_x[0], m_x[1], w_a_x, tm=256, tn=w_a_x.shape[1],
                          out_dtype=F32, cos=cos_x, sin=sin_x)
    pa_c, _ = _norm_proj(ctx2, norm1_g[0], m_c[0], m_c[1], w_a_c, tm=CTX_LEN, tn=w_a_c.shape[1],
                         out_dtype=F32, cos=cos_c, sin=sin_c)
    pconv = _matmul(hx, w_conv, tm=1024, tn=1024, out_dtype=F32)
    conv = _short_conv(pconv, conv_w[0], conv_b[0])

    q = _q_proj(pa_x, q_norm_g[0], wq, cos_x, sin_x, tm=256)
    ckv = jnp.concatenate([pa_x[:, O_CKV:O_KR], pa_c[:, :KV_LORA]], axis=0)
    krot = jnp.concatenate([pa_x[:, O_KR:], pa_c[:, KV_LORA:]], axis=0)
    k, vt = _kv_proj(ckv, kv_norm_g[0], krot, wk, wvt, tm=384)
    att = _attention(q, k, vt, tq=512)

    x1 = _out_proj(att, conv, w_o[0], x2, m_x[2], tm=1024, tn=512)

    qp, h2t = _norm_proj(x1, norm2_g[0], m_x[3], m_x[4], w_pq, tm=512, tn=1024,
                         out_dtype=BF16, transpose_h=True)
    tabs = _peer_gates(qp, keys, tt=512, na=PEER_TE // N_KEYS)
    peer_t = _peer_dense(h2t, u_bf, vt_bf, tabs)
    out = _final(peer_t, x1, m_x[5], final_norm_g, tf=256)
    return out[None]
```

```python
import functools
import math

import numpy as np
import jax
import jax.numpy as jnp
from jax import lax
from jax.experimental import pallas as pl
from jax.experimental.pallas import tpu as pltpu

F32 = jnp.float32
BF16 = jnp.bfloat16

D_MODEL = 4096
SEQ = 8192
GRID_W = 64
CTX_LEN = 256
MLA_HEADS = 16
QK_NOPE = 128
QK_ROPE = 64
V_HEAD = 128
Q_LORA = 768
KV_LORA = 512
ROPE_THETA = 10000.0
SOFTMAX_SCALE = 1.0 / math.sqrt(QK_NOPE + QK_ROPE)
CONV_CH = 2048
PEER_HEADS = 8
N_KEYS = 128
N_EXPERTS = N_KEYS * N_KEYS
PEER_TOPK = 16
PEER_DK = 256
N_MOD = 6
EPS = 1e-6
O_CKV = Q_LORA
O_KR = Q_LORA + KV_LORA
O_CONV = O_KR + QK_ROPE

LANE = 128
BF16_SUBLANES = 16
HEAD_PAD = 256
KV_LEN = SEQ + CTX_LEN
KV_CHUNK = 1408
Q_SCALE = SOFTMAX_SCALE * math.log2(math.e)
QK_AHEAD = 2
PEER_TE = 512
PEER_HALF = 512
NEG = float(np.finfo(np.float32).min)
MIB = 1024 * 1024


def _cparams(sem, vmem_mib):
    return pltpu.CompilerParams(dimension_semantics=sem, vmem_limit_bytes=vmem_mib * MIB)


def _rms(x, g):
    ms = jnp.mean(x * x, axis=-1, keepdims=True)
    return x * lax.rsqrt(ms + EPS) * g


def _mods_kernel(cond_ref, w_ref, b_ref, o_ref):
    c = cond_ref[...]
    s = c * (1.0 / (1.0 + jnp.exp(-c)))
    o_ref[...] = jnp.dot(s.astype(BF16), w_ref[...].astype(BF16),
                         preferred_element_type=F32) + b_ref[...]


def _mods(cond8, w, b):
    n = w.shape[1]
    tn = 512
    return pl.pallas_call(
        _mods_kernel,
        grid=(n // tn,),
        in_specs=[pl.BlockSpec((8, D_MODEL), lambda j: (0, 0)),
                  pl.BlockSpec((D_MODEL, tn), lambda j: (0, j)),
                  pl.BlockSpec((1, tn), lambda j: (0, j))],
        out_specs=pl.BlockSpec((8, tn), lambda j: (0, j)),
        out_shape=jax.ShapeDtypeStruct((8, n), F32),
        compiler_params=_cparams(("arbitrary",), 48),
        name="mods",
    )(cond8, w, b.reshape(1, n))


def _norm_proj_kernel(*refs, rope_last, transpose_h, row_chunk):
    if rope_last:
        x_ref, g_ref, sh_ref, sc_ref, w_ref, cos_ref, sin_ref, o_ref, h_ref, h_scr = refs
    else:
        x_ref, g_ref, sh_ref, sc_ref, w_ref, o_ref, h_ref, h_scr = refs
    tm = x_ref.shape[0]

    @pl.when(pl.program_id(1) == 0)
    def _():
        for r in range(tm // row_chunk):
            rows = slice(r * row_chunk, (r + 1) * row_chunk)
            y = _rms(x_ref[rows, :], g_ref[...])
            h = y * (1.0 + sc_ref[...]) + sh_ref[...]
            hb = h.astype(BF16)
            h_scr[rows, :] = hb
            if transpose_h:
                h_ref[0, :, rows] = h.T.astype(BF16)
            else:
                h_ref[rows, :] = hb

    acc = jnp.dot(h_scr[...], w_ref[...], preferred_element_type=F32)
    if rope_last:
        n = acc.shape[1]
        o_ref[:, : n - LANE] = acc[:, : n - LANE].astype(o_ref.dtype)
        xr = acc[:, n - LANE:]
        o_ref[:, n - LANE:] = (xr * cos_ref[...]
                               + pltpu.roll(xr, LANE // 2, axis=1) * sin_ref[...]).astype(o_ref.dtype)
    else:
        o_ref[...] = acc.astype(o_ref.dtype)


def _norm_proj(x, g, shift, scale, w, *, tm, tn, out_dtype, cos=None, sin=None,
               transpose_h=False, vmem_mib=56):
    m, k = x.shape
    n = w.shape[1]
    rope_last = cos is not None
    row = lambda a: a.reshape(1, k)
    in_specs = [pl.BlockSpec((tm, k), lambda i, j: (i, 0)),
                pl.BlockSpec((1, k), lambda i, j: (0, 0)),
                pl.BlockSpec((1, k), lambda i, j: (0, 0)),
                pl.BlockSpec((1, k), lambda i, j: (0, 0)),
                pl.BlockSpec((k, tn), lambda i, j: (0, j))]
    args = [x, row(g), row(shift), row(scale), w]
    if rope_last:
        assert tn == n
        in_specs += [pl.BlockSpec((tm, LANE), lambda i, j: (i, 0)),
                     pl.BlockSpec((tm, LANE), lambda i, j: (i, 0))]
        args += [cos, sin]
    if transpose_h:
        h_shape, h_spec = (m // tm, k, tm), pl.BlockSpec((1, k, tm), lambda i, j: (i, 0, 0))
    else:
        h_shape, h_spec = (m, k), pl.BlockSpec((tm, k), lambda i, j: (i, 0))
    return pl.pallas_call(
        functools.partial(_norm_proj_kernel, rope_last=rope_last, transpose_h=transpose_h,
                          row_chunk=min(tm, LANE)),
        grid=(m // tm, n // tn),
        in_specs=in_specs,
        out_specs=[pl.BlockSpec((tm, tn), lambda i, j: (i, j)), h_spec],
        out_shape=[jax.ShapeDtypeStruct((m, n), out_dtype),
                   jax.ShapeDtypeStruct(h_shape, BF16)],
        scratch_shapes=[pltpu.VMEM((tm, k), BF16)],
        compiler_params=_cparams(("arbitrary", "arbitrary"), vmem_mib),
        name="norm_proj",
    )(*args)


def _matmul_kernel(a_ref, w_ref, o_ref):
    o_ref[...] = jnp.dot(a_ref[...], w_ref[...], preferred_element_type=F32).astype(o_ref.dtype)


def _matmul(a, w, *, tm, tn, out_dtype, vmem_mib=48):
    m, k = a.shape
    n = w.shape[1]
    return pl.pallas_call(
        _matmul_kernel,
        grid=(m // tm, n // tn),
        in_specs=[pl.BlockSpec((tm, k), lambda i, j: (i, 0)),
                  pl.BlockSpec((k, tn), lambda i, j: (0, j))],
        out_specs=pl.BlockSpec((tm, tn), lambda i, j: (i, j)),
        out_shape=jax.ShapeDtypeStruct((m, n), out_dtype),
        compiler_params=_cparams(("arbitrary", "arbitrary"), vmem_mib),
        name="matmul",
    )(a, w)


def _conv_kernel(xin_ref, gb_ref, gc_ref, w_ref, b_ref, o_ref):
    t = xin_ref.shape[0]
    u = gc_ref[...] * xin_ref[...]
    row = lax.broadcasted_iota(jnp.int32, u.shape, 0)
    prev = jnp.where(row == 0, 0.0, pltpu.roll(u, 1, axis=0))
    nxt = jnp.where(row == t - 1, 0.0, pltpu.roll(u, t - 1, axis=0))
    y = prev * w_ref[0:1, :] + u * w_ref[1:2, :] + nxt * w_ref[2:3, :] + b_ref[...]
    o_ref[...] = (gb_ref[...] * y).astype(o_ref.dtype)


def _short_conv(pconv, w, b):
    t = pconv.shape[0]
    nb = CONV_CH // LANE
    return pl.pallas_call(
        _conv_kernel,
        grid=(nb,),
        in_specs=[pl.BlockSpec((t, LANE), lambda j: (0, j)),
                  pl.BlockSpec((t, LANE), lambda j: (0, nb + j)),
                  pl.BlockSpec((t, LANE), lambda j: (0, 2 * nb + j)),
                  pl.BlockSpec((3, LANE), lambda j: (0, j)),
                  pl.BlockSpec((1, LANE), lambda j: (0, j))],
        out_specs=pl.BlockSpec((t, LANE), lambda j: (0, j)),
        out_shape=jax.ShapeDtypeStruct((t, CONV_CH), BF16),
        compiler_params=_cparams(("arbitrary",), 56),
        name="short_conv",
    )(pconv, pconv, pconv, w, b.reshape(1, CONV_CH))


def _q_proj_kernel(cq_ref, g_ref, w_ref, cos_ref, sin_ref, q_ref):
    y = _rms(cq_ref[...], g_ref[...]).astype(BF16)
    acc = jnp.dot(y, w_ref[...], preferred_element_type=F32)
    cos = cos_ref[...]
    sin = sin_ref[...]
    for h in range(MLA_HEADS):
        lo = h * HEAD_PAD
        q_ref[h, :, :LANE] = (acc[:, lo:lo + LANE] * Q_SCALE).astype(BF16)
        xr = acc[:, lo + LANE:lo + HEAD_PAD]
        rope = xr * cos + pltpu.roll(xr, LANE // 2, axis=1) * sin
        q_ref[h, :, LANE:] = (rope * Q_SCALE).astype(BF16)


def _q_proj(pa, g, w, cos, sin, *, tm):
    m = pa.shape[0]
    n = w.shape[1]
    return pl.pallas_call(
        _q_proj_kernel,
        grid=(m // tm,),
        in_specs=[pl.BlockSpec((tm, Q_LORA), lambda i: (i, 0)),
                  pl.BlockSpec((1, Q_LORA), lambda i: (0, 0)),
                  pl.BlockSpec((Q_LORA, n), lambda i: (0, 0)),
                  pl.BlockSpec((tm, LANE), lambda i: (i, 0)),
                  pl.BlockSpec((tm, LANE), lambda i: (i, 0))],
        out_specs=pl.BlockSpec((MLA_HEADS, tm, HEAD_PAD), lambda i: (0, i, 0)),
        out_shape=jax.ShapeDtypeStruct((MLA_HEADS, m, HEAD_PAD), BF16),
        compiler_params=_cparams(("arbitrary",), 48),
        name="q_proj",
    )(pa, g.reshape(1, Q_LORA), w, cos, sin)


def _kv_proj_kernel(ckv_ref, g_ref, kr_ref, wk_ref, wvt_ref, k_ref, vt_ref):
    y = _rms(ckv_ref[...], g_ref[...]).astype(BF16)
    kn = jnp.dot(y, wk_ref[...], preferred_element_type=F32)
    vt_ref[...] = lax.dot_general(wvt_ref[...], y, (((1,), (1,)), ((), ())),
                                  preferred_element_type=F32).astype(BF16)
    kr = kr_ref[...].astype(BF16)
    for h in range(MLA_HEADS):
        k_ref[h, :, :LANE] = kn[:, h * QK_NOPE:(h + 1) * QK_NOPE].astype(BF16)
        k_ref[h, :, LANE:] = kr


def _kv_proj(ckv, g, krot, wk, wvt, *, tm):
    m = ckv.shape[0]
    nv = MLA_HEADS * V_HEAD
    return pl.pallas_call(
        _kv_proj_kernel,
        grid=(m // tm,),
        in_specs=[pl.BlockSpec((tm, KV_LORA), lambda i: (i, 0)),
                  pl.BlockSpec((1, KV_LORA), lambda i: (0, 0)),
                  pl.BlockSpec((tm, LANE), lambda i: (i, 0)),
                  pl.BlockSpec(wk.shape, lambda i: (0, 0)),
                  pl.BlockSpec(wvt.shape, lambda i: (0, 0))],
        out_specs=[pl.BlockSpec((MLA_HEADS, tm, HEAD_PAD), lambda i: (0, i, 0)),
                   pl.BlockSpec((nv, tm), lambda i: (0, i))],
        out_shape=[jax.ShapeDtypeStruct((MLA_HEADS, m, HEAD_PAD), BF16),
                   jax.ShapeDtypeStruct((nv, m), BF16)],
        compiler_params=_cparams(("arbitrary",), 48),
        name="kv_proj",
    )(ckv, g.reshape(1, KV_LORA), krot, wk, wvt)


def _attn_kernel(q_ref, k_ref, vt_ref, o_ref):
    chunk = KV_CHUNK
    n_chunks = vt_ref.shape[1] // chunk
    q = q_ref[0]
    tq = q.shape[0]
    nt = (((1,), (1,)), ((), ()))
    m = jnp.full((1, tq), NEG, F32)
    l = jnp.zeros((1, tq), F32)
    acc = jnp.zeros((V_HEAD, tq), F32)

    def scores(c):
        return lax.dot_general(k_ref[0, c * chunk:(c + 1) * chunk, :], q, nt,
                               preferred_element_type=F32)

    pending = [scores(c) for c in range(min(QK_AHEAD, n_chunks))]
    for c in range(n_chunks):
        s = pending.pop(0)
        if c + QK_AHEAD < n_chunks:
            pending.append(scores(c + QK_AHEAD))
        m_new = jnp.maximum(m, s.max(axis=0, keepdims=True))
        p = jnp.exp2(s - m_new)
        alpha = jnp.exp2(m - m_new)
        l = alpha * l + p.sum(axis=0, keepdims=True)
        acc = alpha * acc + jnp.dot(vt_ref[:, c * chunk:(c + 1) * chunk], p.astype(BF16),
                                    preferred_element_type=F32)
        m = m_new
    o_ref[...] = (acc / l).T.astype(o_ref.dtype)


def _attention(q, k, vt, *, tq):
    s = q.shape[1]
    kv = k.shape[1]
    assert kv % KV_CHUNK == 0
    return pl.pallas_call(
        _attn_kernel,
        grid=(MLA_HEADS, s // tq),
        in_specs=[pl.BlockSpec((1, tq, HEAD_PAD), lambda h, i: (h, i, 0)),
                  pl.BlockSpec((1, kv, HEAD_PAD), lambda h, i: (h, 0, 0)),
                  pl.BlockSpec((V_HEAD, kv), lambda h, i: (h, 0))],
        out_specs=pl.BlockSpec((tq, V_HEAD), lambda h, i: (i, h)),
        out_shape=jax.ShapeDtypeStruct((s, MLA_HEADS * V_HEAD), BF16),
        compiler_params=_cparams(("arbitrary", "arbitrary"), 48),
        name="attention",
    )(q, k, vt)


def _out_proj_kernel(att_ref, conv_ref, wa_ref, wc_ref, x_ref, gate_ref, o_ref):
    acc = jnp.dot(att_ref[...], wa_ref[...].astype(BF16), preferred_element_type=F32)
    acc = acc + jnp.dot(conv_ref[...], wc_ref[...].astype(BF16), preferred_element_type=F32)
    o_ref[...] = x_ref[...] + gate_ref[...] * acc


def _out_proj(att, conv, w, x, gate, *, tm, tn):
    m, ka = att.shape
    kc = conv.shape[1]
    assert ka == kc and w.shape[0] == ka + kc
    n = w.shape[1]
    return pl.pallas_call(
        _out_proj_kernel,
        grid=(m // tm, n // tn),
        in_specs=[pl.BlockSpec((tm, ka), lambda i, j: (i, 0)),
                  pl.BlockSpec((tm, kc), lambda i, j: (i, 0)),
                  pl.BlockSpec((ka, tn), lambda i, j: (0, j)),
                  pl.BlockSpec((kc, tn), lambda i, j: (1, j)),
                  pl.BlockSpec((tm, tn), lambda i, j: (i, j)),
                  pl.BlockSpec((1, tn), lambda i, j: (0, j))],
        out_specs=pl.BlockSpec((tm, tn), lambda i, j: (i, j)),
        out_shape=jax.ShapeDtypeStruct((m, n), F32),
        compiler_params=_cparams(("arbitrary", "arbitrary"), 56),
        name="out_proj",
    )(att, conv, w, w, x, gate.reshape(1, n))


def _top16(pieces, lanes, want_rank=False):
    rid = lax.broadcasted_iota(jnp.int32, (PEER_TOPK, lanes), 0)
    tops = jnp.zeros((PEER_TOPK, lanes), F32)
    ranks = [jnp.full(p.shape, float(PEER_TOPK), F32) for p in pieces]
    for k in range(PEER_TOPK):
        slabs = [p[r:r + 8] for p in pieces for r in range(0, p.shape[0], 8)]
        m = functools.reduce(jnp.maximum, slabs).max(axis=0, keepdims=True)
        tops = jnp.where(rid == k, m, tops)
        if want_rank or k + 1 < PEER_TOPK:
            hit = [p == m for p in pieces]
            if want_rank:
                ranks = [jnp.where(e, float(k), r) for e, r in zip(hit, ranks)]
            pieces = [jnp.where(e, NEG, p) for e, p in zip(hit, pieces)]
    return (tops, ranks) if want_rank else tops


def _peer_gate_kernel(q_ref, keys_ref, c1_ref, e1_ref, r2_ref, e2_ref):
    lanes = q_ref.shape[0]
    n_groups, na = c1_ref.shape[1], c1_ref.shape[3]
    nt = (((1,), (1,)), ((), ()))
    for h in range(PEER_HEADS):
        q1 = q_ref[:, (2 * h) * LANE:(2 * h + 1) * LANE]
        q2 = q_ref[:, (2 * h + 1) * LANE:(2 * h + 2) * LANE]
        s1_all = lax.dot_general(keys_ref[2 * h], q1, nt, preferred_element_type=F32)
        s2_all = lax.dot_general(keys_ref[2 * h + 1], q2, nt, preferred_element_type=F32)
        for c in range(lanes // LANE):
            cols = slice(c * LANE, (c + 1) * LANE)
            s1 = s1_all[:, cols]
            s2 = s2_all[:, cols]
            t1 = _top16([s1], LANE)
            t2, (rank2,) = _top16([s2], LANE, want_rank=True)
            cands = [t1[0:1] + t2]
            cands += [t1[i:i + 1] + t2[0:8] for i in range(1, 8)]
            cands += [t1[8:16] + t2[0:1]]
            top = _top16(cands, LANE)
            thr = top[PEER_TOPK - 1:PEER_TOPK]
            z = jnp.sum(jnp.exp(top - top[0:1]), axis=0, keepdims=True)
            count1 = jnp.zeros(s1.shape, F32)
            for i in range(PEER_TOPK):
                n_pass = jnp.sum(jnp.where(t1[i:i + 1] + t2 >= thr, 1.0, 0.0), axis=0,
                                 keepdims=True)
                count1 = jnp.where(s1 == t1[i:i + 1], n_pass, count1)
            e1 = jnp.exp(s1 - t1[0:1]) / z
            for r in range(n_groups):
                c1_ref[0, r, h, :, cols] = count1[r * na:(r + 1) * na, :]
                e1_ref[0, r, h, :, cols] = e1[r * na:(r + 1) * na, :]
            r2_ref[0, h, :, cols] = rank2.astype(BF16)
            e2_ref[0, h, :, cols] = jnp.exp(s2 - t2[0:1]).astype(BF16)


def _peer_gates(q, keys, *, tt, na):
    t = q.shape[0]
    nb = t // tt
    row_shape = (nb, N_KEYS // na, PEER_HEADS, na, tt)
    tab_shape = (nb, PEER_HEADS, N_KEYS, tt)
    blk = lambda shape: pl.BlockSpec((1,) + shape[1:], lambda i: (i,) + (0,) * (len(shape) - 1))
    return pl.pallas_call(
        _peer_gate_kernel,
        grid=(nb,),
        in_specs=[pl.BlockSpec((tt, PEER_HEADS * PEER_DK), lambda i: (i, 0)),
                  pl.BlockSpec(keys.shape, lambda i: (0, 0, 0))],
        out_specs=[blk(row_shape), blk(row_shape), blk(tab_shape), blk(tab_shape)],
        out_shape=[jax.ShapeDtypeStruct(row_shape, F32), jax.ShapeDtypeStruct(row_shape, F32),
                   jax.ShapeDtypeStruct(tab_shape, BF16), jax.ShapeDtypeStruct(tab_shape, BF16)],
        compiler_params=_cparams(("arbitrary",), 48),
        name="peer_gates",
    )(q, keys)


def _slab_kernel(v_ref, o_ref):
    o_ref[0] = v_ref[...].T.astype(BF16)


def _expert_slabs(v, *, te):
    ne, d = v.shape
    return pl.pallas_call(
        _slab_kernel,
        grid=(ne // te,),
        in_specs=[pl.BlockSpec((te, d), lambda i: (i, 0))],
        out_specs=pl.BlockSpec((1, d, te), lambda i: (i, 0, 0)),
        out_shape=jax.ShapeDtypeStruct((ne // te, d, te), BF16),
        compiler_params=_cparams(("arbitrary",), 48),
        name="expert_slabs",
    )(v)


def _row_tile_bf16(row):
    return jnp.broadcast_to(row, (BF16_SUBLANES, LANE)).astype(BF16)[None]


def _peer_kernel(ht_ref, u_ref, vt_ref, c1_ref, e1_ref, r2_ref, e2_ref, o_ref,
                 act_a, act_b, w_a, *, n_tiles):
    j = pl.program_id(0)
    te, tm = w_a.shape

    @pl.when(j == 0)
    def _():
        act_a[...] = jnp.zeros(act_a.shape, F32)

    @pl.when(jnp.maximum(j - 1, 0) % n_tiles == 0)
    def _():
        o_ref[...] = jnp.zeros(o_ref.shape, F32)

    def step(act_rd, act_wr):
        act_wr[...] = jnp.dot(u_ref[...], ht_ref[0], preferred_element_type=F32)
        for half in range(te // PEER_HALF):
            for a in range(half * (PEER_HALF // N_KEYS), (half + 1) * (PEER_HALF // N_KEYS)):
                rows = slice(a * N_KEYS, (a + 1) * N_KEYS)
                for c in range(tm // LANE):
                    cols = slice(c * LANE, (c + 1) * LANE)
                    act = act_rd[rows, cols]
                    gel = 0.5 * act * (1.0 + lax.erf(act * (1.0 / math.sqrt(2.0))))
                    gshape = (N_KEYS // BF16_SUBLANES, BF16_SUBLANES, LANE)
                    g = jnp.zeros(gshape, BF16)
                    for h in range(PEER_HEADS):
                        count = _row_tile_bf16(c1_ref[0, 0, h, a:a + 1, cols])
                        e1row = _row_tile_bf16(e1_ref[0, 0, h, a:a + 1, cols])
                        sel = jnp.where(r2_ref[0, h, :, cols].reshape(gshape) < count,
                                        e2_ref[0, h, :, cols].reshape(gshape),
                                        jnp.zeros(gshape, BF16))
                        g = g + e1row * sel
                    w_a[rows, cols] = g.reshape(N_KEYS, LANE) * gel.astype(BF16)
            hrows = slice(half * PEER_HALF, (half + 1) * PEER_HALF)
            o_ref[0] += jnp.dot(vt_ref[0, :, hrows], w_a[hrows, :], preferred_element_type=F32)

    @pl.when(j % 2 == 0)
    def _():
        step(act_a, act_b)

    @pl.when(j % 2 == 1)
    def _():
        step(act_b, act_a)


def _peer_dense(ht, u, vt, tabs):
    nb, d, tm = ht.shape
    n_tiles, _, te = vt.shape
    c1, e1, r2, e2 = tabs
    na = te // N_KEYS
    assert c1.shape == (nb, N_KEYS // na, PEER_HEADS, na, tm)
    n_pairs = nb * n_tiles
    cur = lambda g: jnp.minimum(g, n_pairs - 1)
    prv = lambda g: jnp.maximum(g - 1, 0)
    row_spec = pl.BlockSpec((1, 1, PEER_HEADS, na, tm),
                            lambda g: (prv(g) // n_tiles, prv(g) % n_tiles, 0, 0, 0))
    tab_spec = pl.BlockSpec((1, PEER_HEADS, N_KEYS, tm), lambda g: (prv(g) // n_tiles, 0, 0, 0))
    return pl.pallas_call(
        functools.partial(_peer_kernel, n_tiles=n_tiles),
        grid=(n_pairs + 1,),
        in_specs=[pl.BlockSpec((1, d, tm), lambda g: (cur(g) // n_tiles, 0, 0)),
                  pl.BlockSpec((te, d), lambda g: (cur(g) % n_tiles, 0)),
                  pl.BlockSpec((1, d, te), lambda g: (prv(g) % n_tiles, 0, 0)),
                  row_spec, row_spec, tab_spec, tab_spec],
        out_specs=pl.BlockSpec((1, d, tm), lambda g: (prv(g) // n_tiles, 0, 0)),
        out_shape=jax.ShapeDtypeStruct((nb, d, tm), F32),
        scratch_shapes=[pltpu.VMEM((te, tm), F32), pltpu.VMEM((te, tm), F32),
                        pltpu.VMEM((te, tm), BF16)],
        compiler_params=_cparams(("arbitrary",), 60),
        name="peer_dense",
    )(ht, u, vt, c1, e1, r2, e2)


def _final_kernel(pt_ref, x_ref, gate_ref, g_ref, o_ref):
    x2 = x_ref[...] + gate_ref[...] * pt_ref[0].T
    o_ref[...] = _rms(x2, g_ref[...])


def _final(pt, x1, gate, g, *, tf):
    nb, d, tm = pt.shape
    t = nb * tm
    per = tm // tf
    return pl.pallas_call(
        _final_kernel,
        grid=(t // tf,),
        in_specs=[pl.BlockSpec((1, d, tf), lambda i: (i // per, 0, i % per)),
                  pl.BlockSpec((tf, d), lambda i: (i, 0)),
                  pl.BlockSpec((1, d), lambda i: (0, 0)),
                  pl.BlockSpec((1, d), lambda i: (0, 0))],
        out_specs=pl.BlockSpec((tf, d), lambda i: (i, 0)),
        out_shape=jax.ShapeDtypeStruct((t, d), F32),
        compiler_params=_cparams(("arbitrary",), 48),
        name="final",
    )(pt, x1, gate.reshape(1, d), g.reshape(1, d))


def _rope_perm():
    j = np.arange(QK_ROPE)
    return np.where((j % 32) < 16, j + 16, j - 16)


def _rope_tables(t):
    f32 = np.float32
    rows = t // GRID_W
    row = np.repeat(np.arange(rows), GRID_W).astype(f32)
    col = np.tile(np.arange(GRID_W), rows).astype(f32)
    half = QK_ROPE // 2
    freqs = (f32(ROPE_THETA) ** (-np.arange(0, half, 2, dtype=f32) / f32(half))).astype(f32)
    ar = (row[:, None] * freqs).astype(np.float64)
    ac = (col[:, None] * freqs).astype(np.float64)
    zeros = np.zeros((t, LANE - QK_ROPE))
    cos = np.concatenate([np.cos(ar), np.cos(ar), np.cos(ac), np.cos(ac), zeros], axis=1)
    sin = np.concatenate([-np.sin(ar), np.sin(ar), -np.sin(ac), np.sin(ac), zeros], axis=1)
    return jnp.asarray(cos.astype(f32)), jnp.asarray(sin.astype(f32))


def kernel(x, c, ctx, c_ctx, w_ada, b_ada, norm1_g, w_in, q_norm_g, w_uq, kv_norm_g, w_ukv,
           conv_w, conv_b, w_o, norm2_g, peer_wq, peer_keys, peer_u, peer_v, final_norm_g):
    assert x.shape == (1, SEQ, D_MODEL) and ctx.shape == (1, CTX_LEN, D_MODEL)
    assert w_ada.shape[0] == 1
    x2 = x[0]
    ctx2 = ctx[0]
    perm = _rope_perm()

    w_in0 = w_in[0]
    w_kr_perm = w_in0[:, O_KR:O_CONV][:, perm]
    w_a_x = jnp.concatenate([w_in0[:, :O_CONV], w_kr_perm], axis=1).astype(BF16)
    w_a_c = jnp.concatenate([w_in0[:, O_CKV:O_CONV], w_kr_perm], axis=1).astype(BF16)
    w_conv = w_in0[:, O_CONV:].astype(BF16)
    wq = w_uq[0].reshape(Q_LORA, MLA_HEADS, QK_NOPE + QK_ROPE)
    wq = jnp.concatenate([wq, wq[:, :, QK_NOPE:][:, :, perm]], axis=2)
    wq = wq.reshape(Q_LORA, MLA_HEADS * HEAD_PAD).astype(BF16)
    wkv = w_ukv[0].reshape(KV_LORA, MLA_HEADS, QK_NOPE + V_HEAD)
    wk = wkv[:, :, :QK_NOPE].reshape(KV_LORA, MLA_HEADS * QK_NOPE).astype(BF16)
    wvt = wkv[:, :, QK_NOPE:].reshape(KV_LORA, MLA_HEADS * V_HEAD).T.astype(BF16)
    w_pq = peer_wq[0].astype(BF16)
    keys = peer_keys[0].reshape(PEER_HEADS * 2, N_KEYS, PEER_DK // 2).astype(BF16)
    u_bf = peer_u[0].astype(BF16)
    vt_bf = _expert_slabs(peer_v[0], te=PEER_TE)

    cos_x, sin_x = _rope_tables(SEQ)
    cos_c = jnp.concatenate([jnp.ones((CTX_LEN, QK_ROPE), F32),
                             jnp.zeros((CTX_LEN, LANE - QK_ROPE), F32)], axis=1)
    sin_c = jnp.zeros((CTX_LEN, LANE), F32)

    cond8 = jnp.concatenate([c, c_ctx[None, :], jnp.zeros((6, D_MODEL), F32)], axis=0)
    mods = _mods(cond8, w_ada[0], b_ada[0])
    m_x = mods[0].reshape(N_MOD, D_MODEL)
    m_c = mods[1].reshape(N_MOD, D_MODEL)

    pa_x, hx = _norm_proj(x2, norm1_g[0], m_x[0], m_x[1], w_a_x, tm=256, tn=w_a_x.shape[1],
                          out_dtype=F32, cos=cos_x, sin=sin_x)
    pa_c, _ = _norm_proj(ctx2, norm1_g[0], m_c[0], m_c[1], w_a_c, tm=CTX_LEN, tn=w_a_c.shape[1],
                         out_dtype=F32, cos=cos_c, sin=sin_c)
    pconv = _matmul(hx, w_conv, tm=1024, tn=1024, out_dtype=F32)
    conv = _short_conv(pconv, conv_w[0], conv_b[0])

    q = _q_proj(pa_x, q_norm_g[0], wq, cos_x, sin_x, tm=256)
    ckv = jnp.concatenate([pa_x[:, O_CKV:O_KR], pa_c[:, :KV_LORA]], axis=0)
    krot = jnp.concatenate([pa_x[:, O_KR:], pa_c[:, KV_LORA:]], axis=0)
    k, vt = _kv_proj(ckv, kv_norm_g[0], krot, wk, wvt, tm=384)
    att = _attention(q, k, vt, tq=512)

    x1 = _out_proj(att, conv, w_o[0], x2, m_x[2], tm=1024, tn=512)

    qp, h2t = _norm_proj(x1, norm2_g[0], m_x[3], m_x[4], w_pq, tm=512, tn=1024,
                         out_dtype=BF16, transpose_h=True)
    tabs = _peer_gates(qp, keys, tt=512, na=PEER_TE // N_KEYS)
    peer_t = _peer_dense(h2t, u_bf, vt_bf, tabs)
    out = _final(peer_t, x1, m_x[5], final_norm_g, tf=256)
    return out[None]
```

```python
import functools
import math

import numpy as np
import jax
import jax.numpy as jnp
from jax import lax
from jax.experimental import pallas as pl
from jax.experimental.pallas import tpu as pltpu

F32 = jnp.float32
BF16 = jnp.bfloat16

D_MODEL = 4096
SEQ = 8192
GRID_W = 64
CTX_LEN = 256
MLA_HEADS = 16
QK_NOPE = 128
QK_ROPE = 64
V_HEAD = 128
Q_LORA = 768
KV_LORA = 512
ROPE_THETA = 10000.0
SOFTMAX_SCALE = 1.0 / math.sqrt(QK_NOPE + QK_ROPE)
CONV_CH = 2048
PEER_HEADS = 8
N_KEYS = 128
N_EXPERTS = N_KEYS * N_KEYS
PEER_TOPK = 16
PEER_DK = 256
N_MOD = 6
EPS = 1e-6
O_CKV = Q_LORA
O_KR = Q_LORA + KV_LORA
O_CONV = O_KR + QK_ROPE

LANE = 128
F32_SUBLANES = 8
BF16_SUBLANES = 16
VMEM_MIB = 64
MIB = 1024 * 1024

HEAD_PAD = 256
KV_LEN = SEQ + CTX_LEN
KV_CHUNK = 1408
Q_SCALE = SOFTMAX_SCALE * math.log2(math.e)
QK_AHEAD = 2
PEER_TE = 512
NEG = float(np.finfo(np.float32).min)

TILES = dict(
    mods_tn=512,
    in_proj_tm=256,
    conv_proj_tm=1024, conv_proj_tn=1024,
    q_proj_tm=256,
    kv_proj_tm=384,
    attn_tq=512,
    out_proj_tm=1024, out_proj_tn=512,
    peer_tm=512,
    peer_q_tn=1024,
    final_tf=256,
)
VMEM_SMALL = 48
VMEM_LARGE = 56
VMEM_PEER = VMEM_MIB - 4


def _cparams(sem, vmem_mib):
    assert vmem_mib <= VMEM_MIB
    return pltpu.CompilerParams(dimension_semantics=sem, vmem_limit_bytes=vmem_mib * MIB)


def _rms(x, g):
    ms = jnp.mean(x * x, axis=-1, keepdims=True)
    return x * lax.rsqrt(ms + EPS) * g


def _mods_kernel(cond_ref, w_ref, b_ref, o_ref):
    c = cond_ref[...]
    s = c * (1.0 / (1.0 + jnp.exp(-c)))
    o_ref[...] = jnp.dot(s.astype(BF16), w_ref[...].astype(BF16),
                         preferred_element_type=F32) + b_ref[...]


def _mods(cond8, w, b):
    rows = cond8.shape[0]
    n = w.shape[1]
    tn = TILES["mods_tn"]
    return pl.pallas_call(
        _mods_kernel,
        grid=(n // tn,),
        in_specs=[pl.BlockSpec((rows, D_MODEL), lambda j: (0, 0)),
                  pl.BlockSpec((D_MODEL, tn), lambda j: (0, j)),
                  pl.BlockSpec((1, tn), lambda j: (0, j))],
        out_specs=pl.BlockSpec((rows, tn), lambda j: (0, j)),
        out_shape=jax.ShapeDtypeStruct((rows, n), F32),
        compiler_params=_cparams(("arbitrary",), VMEM_SMALL),
        name="mods",
    )(cond8, w, b.reshape(1, n))


def _norm_proj_kernel(*refs, rope_last, transpose_h, row_chunk):
    if rope_last:
        x_ref, g_ref, sh_ref, sc_ref, w_ref, cos_ref, sin_ref, o_ref, h_ref, h_scr = refs
    else:
        x_ref, g_ref, sh_ref, sc_ref, w_ref, o_ref, h_ref, h_scr = refs
    tm = x_ref.shape[0]

    @pl.when(pl.program_id(1) == 0)
    def _():
        for r in range(tm // row_chunk):
            rows = slice(r * row_chunk, (r + 1) * row_chunk)
            y = _rms(x_ref[rows, :], g_ref[...])
            h = y * (1.0 + sc_ref[...]) + sh_ref[...]
            hb = h.astype(BF16)
            h_scr[rows, :] = hb
            if transpose_h:
                h_ref[0, :, rows] = h.T.astype(BF16)
            else:
                h_ref[rows, :] = hb

    acc = jnp.dot(h_scr[...], w_ref[...], preferred_element_type=F32)
    if rope_last:
        n = acc.shape[1]
        o_ref[:, : n - LANE] = acc[:, : n - LANE].astype(o_ref.dtype)
        xr = acc[:, n - LANE:]
        o_ref[:, n - LANE:] = (xr * cos_ref[...]
                               + pltpu.roll(xr, LANE // 2, axis=1) * sin_ref[...]).astype(o_ref.dtype)
    else:
        o_ref[...] = acc.astype(o_ref.dtype)


def _norm_proj(x, g, shift, scale, w, *, tm, tn, out_dtype, cos=None, sin=None,
               transpose_h=False, vmem_mib=VMEM_LARGE):
    m, k = x.shape
    n = w.shape[1]
    rope_last = cos is not None
    row = lambda a: a.reshape(1, k)
    in_specs = [pl.BlockSpec((tm, k), lambda i, j: (i, 0)),
                pl.BlockSpec((1, k), lambda i, j: (0, 0)),
                pl.BlockSpec((1, k), lambda i, j: (0, 0)),
                pl.BlockSpec((1, k), lambda i, j: (0, 0)),
                pl.BlockSpec((k, tn), lambda i, j: (0, j))]
    args = [x, row(g), row(shift), row(scale), w]
    if rope_last:
        assert tn == n
        in_specs += [pl.BlockSpec((tm, LANE), lambda i, j: (i, 0)),
                     pl.BlockSpec((tm, LANE), lambda i, j: (i, 0))]
        args += [cos, sin]
    if transpose_h:
        h_shape, h_spec = (m // tm, k, tm), pl.BlockSpec((1, k, tm), lambda i, j: (i, 0, 0))
    else:
        h_shape, h_spec = (m, k), pl.BlockSpec((tm, k), lambda i, j: (i, 0))
    return pl.pallas_call(
        functools.partial(_norm_proj_kernel, rope_last=rope_last, transpose_h=transpose_h,
                          row_chunk=min(tm, LANE)),
        grid=(m // tm, n // tn),
        in_specs=in_specs,
        out_specs=[pl.BlockSpec((tm, tn), lambda i, j: (i, j)), h_spec],
        out_shape=[jax.ShapeDtypeStruct((m, n), out_dtype),
                   jax.ShapeDtypeStruct(h_shape, BF16)],
        scratch_shapes=[pltpu.VMEM((tm, k), BF16)],
        compiler_params=_cparams(("arbitrary", "arbitrary"), vmem_mib),
        name="norm_proj",
    )(*args)


def _matmul_kernel(a_ref, w_ref, o_ref):
    o_ref[...] = jnp.dot(a_ref[...], w_ref[...], preferred_element_type=F32).astype(o_ref.dtype)


def _matmul(a, w, *, tm, tn, out_dtype, vmem_mib=VMEM_SMALL):
    m, k = a.shape
    n = w.shape[1]
    return pl.pallas_call(
        _matmul_kernel,
        grid=(m // tm, n // tn),
        in_specs=[pl.BlockSpec((tm, k), lambda i, j: (i, 0)),
                  pl.BlockSpec((k, tn), lambda i, j: (0, j))],
        out_specs=pl.BlockSpec((tm, tn), lambda i, j: (i, j)),
        out_shape=jax.ShapeDtypeStruct((m, n), out_dtype),
        compiler_params=_cparams(("arbitrary", "arbitrary"), vmem_mib),
        name="matmul",
    )(a, w)


def _conv_kernel(xin_ref, gb_ref, gc_ref, w_ref, b_ref, o_ref):
    t = xin_ref.shape[0]
    u = gc_ref[...] * xin_ref[...]
    row = lax.broadcasted_iota(jnp.int32, u.shape, 0)
    prev = jnp.where(row == 0, 0.0, pltpu.roll(u, 1, axis=0))
    nxt = jnp.where(row == t - 1, 0.0, pltpu.roll(u, t - 1, axis=0))
    y = prev * w_ref[0:1, :] + u * w_ref[1:2, :] + nxt * w_ref[2:3, :] + b_ref[...]
    o_ref[...] = (gb_ref[...] * y).astype(o_ref.dtype)


def _short_conv(pconv, w, b):
    t = pconv.shape[0]
    nb = CONV_CH // LANE
    return pl.pallas_call(
        _conv_kernel,
        grid=(nb,),
        in_specs=[pl.BlockSpec((t, LANE), lambda j: (0, j)),
                  pl.BlockSpec((t, LANE), lambda j: (0, nb + j)),
                  pl.BlockSpec((t, LANE), lambda j: (0, 2 * nb + j)),
                  pl.BlockSpec((3, LANE), lambda j: (0, j)),
                  pl.BlockSpec((1, LANE), lambda j: (0, j))],
        out_specs=pl.BlockSpec((t, LANE), lambda j: (0, j)),
        out_shape=jax.ShapeDtypeStruct((t, CONV_CH), BF16),
        compiler_params=_cparams(("arbitrary",), VMEM_LARGE),
        name="short_conv",
    )(pconv, pconv, pconv, w, b.reshape(1, CONV_CH))


def _q_proj_kernel(cq_ref, g_ref, w_ref, cos_ref, sin_ref, q_ref):
    y = _rms(cq_ref[...], g_ref[...]).astype(BF16)
    acc = jnp.dot(y, w_ref[...], preferred_element_type=F32)
    cos = cos_ref[...]
    sin = sin_ref[...]
    for h in range(MLA_HEADS):
        lo = h * HEAD_PAD
        q_ref[h, :, :LANE] = (acc[:, lo:lo + LANE] * Q_SCALE).astype(BF16)
        xr = acc[:, lo + LANE:lo + HEAD_PAD]
        rope = xr * cos + pltpu.roll(xr, LANE // 2, axis=1) * sin
        q_ref[h, :, LANE:] = (rope * Q_SCALE).astype(BF16)


def _q_proj(pa, g, w, cos, sin, *, tm):
    m = pa.shape[0]
    n = w.shape[1]
    return pl.pallas_call(
        _q_proj_kernel,
        grid=(m // tm,),
        in_specs=[pl.BlockSpec((tm, Q_LORA), lambda i: (i, 0)),
                  pl.BlockSpec((1, Q_LORA), lambda i: (0, 0)),
                  pl.BlockSpec((Q_LORA, n), lambda i: (0, 0)),
                  pl.BlockSpec((tm, LANE), lambda i: (i, 0)),
                  pl.BlockSpec((tm, LANE), lambda i: (i, 0))],
        out_specs=pl.BlockSpec((MLA_HEADS, tm, HEAD_PAD), lambda i: (0, i, 0)),
        out_shape=jax.ShapeDtypeStruct((MLA_HEADS, m, HEAD_PAD), BF16),
        compiler_params=_cparams(("arbitrary",), VMEM_SMALL),
        name="q_proj",
    )(pa, g.reshape(1, Q_LORA), w, cos, sin)


def _kv_proj_kernel(ckv_ref, g_ref, kr_ref, wk_ref, wvt_ref, k_ref, vt_ref):
    y = _rms(ckv_ref[...], g_ref[...]).astype(BF16)
    kn = jnp.dot(y, wk_ref[...], preferred_element_type=F32)
    vt_ref[...] = lax.dot_general(wvt_ref[...], y, (((1,), (1,)), ((), ())),
                                  preferred_element_type=F32).astype(BF16)
    kr = kr_ref[...].astype(BF16)
    for h in range(MLA_HEADS):
        k_ref[h, :, :LANE] = kn[:, h * QK_NOPE:(h + 1) * QK_NOPE].astype(BF16)
        k_ref[h, :, LANE:] = kr


def _kv_proj(ckv, g, krot, wk, wvt, *, tm):
    m = ckv.shape[0]
    nv = MLA_HEADS * V_HEAD
    return pl.pallas_call(
        _kv_proj_kernel,
        grid=(m // tm,),
        in_specs=[pl.BlockSpec((tm, KV_LORA), lambda i: (i, 0)),
                  pl.BlockSpec((1, KV_LORA), lambda i: (0, 0)),
                  pl.BlockSpec((tm, LANE), lambda i: (i, 0)),
                  pl.BlockSpec(wk.shape, lambda i: (0, 0)),
                  pl.BlockSpec(wvt.shape, lambda i: (0, 0))],
        out_specs=[pl.BlockSpec((MLA_HEADS, tm, HEAD_PAD), lambda i: (0, i, 0)),
                   pl.BlockSpec((nv, tm), lambda i: (0, i))],
        out_shape=[jax.ShapeDtypeStruct((MLA_HEADS, m, HEAD_PAD), BF16),
                   jax.ShapeDtypeStruct((nv, m), BF16)],
        compiler_params=_cparams(("arbitrary",), VMEM_SMALL),
        name="kv_proj",
    )(ckv, g.reshape(1, KV_LORA), krot, wk, wvt)


def _attn_kernel(q_ref, k_ref, vt_ref, o_ref):
    chunk = KV_CHUNK
    n_chunks = vt_ref.shape[1] // chunk
    q = q_ref[0]
    tq = q.shape[0]
    nt = (((1,), (1,)), ((), ()))
    m = jnp.full((1, tq), NEG, F32)
    l = jnp.zeros((1, tq), F32)
    acc = jnp.zeros((V_HEAD, tq), F32)

    def scores(c):
        return lax.dot_general(k_ref[0, c * chunk:(c + 1) * chunk, :], q, nt,
                               preferred_element_type=F32)

    pending = [scores(c) for c in range(min(QK_AHEAD, n_chunks))]
    for c in range(n_chunks):
        s = pending.pop(0)
        if c + QK_AHEAD < n_chunks:
            pending.append(scores(c + QK_AHEAD))
        m_new = jnp.maximum(m, s.max(axis=0, keepdims=True))
        p = jnp.exp2(s - m_new)
        alpha = jnp.exp2(m - m_new)
        l = alpha * l + p.sum(axis=0, keepdims=True)
        acc = alpha * acc + jnp.dot(vt_ref[:, c * chunk:(c + 1) * chunk], p.astype(BF16),
                                    preferred_element_type=F32)
        m = m_new
    o_ref[...] = (acc / l).T.astype(o_ref.dtype)


def _attention(q, k, vt, *, tq):
    s = q.shape[1]
    kv = k.shape[1]
    assert kv % KV_CHUNK == 0
    return pl.pallas_call(
        _attn_kernel,
        grid=(MLA_HEADS, s // tq),
        in_specs=[pl.BlockSpec((1, tq, HEAD_PAD), lambda h, i: (h, i, 0)),
                  pl.BlockSpec((1, kv, HEAD_PAD), lambda h, i: (h, 0, 0)),
                  pl.BlockSpec((V_HEAD, kv), lambda h, i: (h, 0))],
        out_specs=pl.BlockSpec((tq, V_HEAD), lambda h, i: (i, h)),
        out_shape=jax.ShapeDtypeStruct((s, MLA_HEADS * V_HEAD), BF16),
        compiler_params=_cparams(("arbitrary", "arbitrary"), VMEM_SMALL),
        name="attention",
    )(q, k, vt)


def _out_proj_kernel(att_ref, conv_ref, wa_ref, wc_ref, x_ref, gate_ref, o_ref):
    acc = jnp.dot(att_ref[...], wa_ref[...].astype(BF16), preferred_element_type=F32)
    acc = acc + jnp.dot(conv_ref[...], wc_ref[...].astype(BF16), preferred_element_type=F32)
    o_ref[...] = x_ref[...] + gate_ref[...] * acc


def _out_proj(att, conv, w, x, gate, *, tm, tn):
    m, ka = att.shape
    kc = conv.shape[1]
    assert ka == kc and w.shape[0] == ka + kc
    n = w.shape[1]
    return pl.pallas_call(
        _out_proj_kernel,
        grid=(m // tm, n // tn),
        in_specs=[pl.BlockSpec((tm, ka), lambda i, j: (i, 0)),
                  pl.BlockSpec((tm, kc), lambda i, j: (i, 0)),
                  pl.BlockSpec((ka, tn), lambda i, j: (0, j)),
                  pl.BlockSpec((kc, tn), lambda i, j: (1, j)),
                  pl.BlockSpec((tm, tn), lambda i, j: (i, j)),
                  pl.BlockSpec((1, tn), lambda i, j: (0, j))],
        out_specs=pl.BlockSpec((tm, tn), lambda i, j: (i, j)),
        out_shape=jax.ShapeDtypeStruct((m, n), F32),
        compiler_params=_cparams(("arbitrary", "arbitrary"), VMEM_LARGE),
        name="out_proj",
    )(att, conv, w, w, x, gate.reshape(1, n))


def _top16(pieces, lanes, want_rank=False):
    rid = lax.broadcasted_iota(jnp.int32, (PEER_TOPK, lanes), 0)
    tops = jnp.zeros((PEER_TOPK, lanes), F32)
    ranks = [jnp.full(p.shape, float(PEER_TOPK), F32) for p in pieces]
    for k in range(PEER_TOPK):
        slabs = [p[r:r + F32_SUBLANES] for p in pieces
                 for r in range(0, p.shape[0], F32_SUBLANES)]
        m = functools.reduce(jnp.maximum, slabs).max(axis=0, keepdims=True)
        tops = jnp.where(rid == k, m, tops)
        if want_rank or k + 1 < PEER_TOPK:
            hit = [p == m for p in pieces]
            if want_rank:
                ranks = [jnp.where(e, float(k), r) for e, r in zip(hit, ranks)]
            pieces = [jnp.where(e, NEG, p) for e, p in zip(hit, pieces)]
    return (tops, ranks) if want_rank else tops


def _peer_gate_kernel(q_ref, keys_ref, c1_ref, e1_ref, r2_ref, e2_ref):
    lanes = q_ref.shape[0]
    n_groups, na = c1_ref.shape[1], c1_ref.shape[3]
    nt = (((1,), (1,)), ((), ()))
    for h in range(PEER_HEADS):
        q1 = q_ref[:, (2 * h) * LANE:(2 * h + 1) * LANE]
        q2 = q_ref[:, (2 * h + 1) * LANE:(2 * h + 2) * LANE]
        s1_all = lax.dot_general(keys_ref[2 * h], q1, nt, preferred_element_type=F32)
        s2_all = lax.dot_general(keys_ref[2 * h + 1], q2, nt, preferred_element_type=F32)
        for c in range(lanes // LANE):
            cols = slice(c * LANE, (c + 1) * LANE)
            s1 = s1_all[:, cols]
            s2 = s2_all[:, cols]
            t1 = _top16([s1], LANE)
            t2, (rank2,) = _top16([s2], LANE, want_rank=True)
            cands = [t1[0:1] + t2]
            cands += [t1[i:i + 1] + t2[0:8] for i in range(1, 8)]
            cands += [t1[8:16] + t2[0:1]]
            top = _top16(cands, LANE)
            thr = top[PEER_TOPK - 1:PEER_TOPK]
            z = jnp.sum(jnp.exp(top - top[0:1]), axis=0, keepdims=True)
            count1 = jnp.zeros(s1.shape, F32)
            for i in range(PEER_TOPK):
                n_pass = jnp.sum(jnp.where(t1[i:i + 1] + t2 >= thr, 1.0, 0.0), axis=0,
                                 keepdims=True)
                count1 = jnp.where(s1 == t1[i:i + 1], n_pass, count1)
            e1 = jnp.exp(s1 - t1[0:1]) / z
            for r in range(n_groups):
                c1_ref[0, r, h, :, cols] = count1[r * na:(r + 1) * na, :]
                e1_ref[0, r, h, :, cols] = e1[r * na:(r + 1) * na, :]
            r2_ref[0, h, :, cols] = rank2.astype(BF16)
            e2_ref[0, h, :, cols] = jnp.exp(s2 - t2[0:1]).astype(BF16)


def _peer_gates(q, keys, *, tt, na):
    t = q.shape[0]
    nb = t // tt
    row_shape = (nb, N_KEYS // na, PEER_HEADS, na, tt)
    tab_shape = (nb, PEER_HEADS, N_KEYS, tt)
    blk = lambda shape: pl.BlockSpec((1,) + shape[1:], lambda i: (i,) + (0,) * (len(shape) - 1))
    return pl.pallas_call(
        _peer_gate_kernel,
        grid=(nb,),
        in_specs=[pl.BlockSpec((tt, PEER_HEADS * PEER_DK), lambda i: (i, 0)),
                  pl.BlockSpec(keys.shape, lambda i: (0, 0, 0))],
        out_specs=[blk(row_shape), blk(row_shape), blk(tab_shape), blk(tab_shape)],
        out_shape=[jax.ShapeDtypeStruct(row_shape, F32), jax.ShapeDtypeStruct(row_shape, F32),
                   jax.ShapeDtypeStruct(tab_shape, BF16), jax.ShapeDtypeStruct(tab_shape, BF16)],
        compiler_params=_cparams(("arbitrary",), VMEM_SMALL),
        name="peer_gates",
    )(q, keys)


def _slab_kernel(v_ref, o_ref):
    o_ref[0] = v_ref[...].T.astype(BF16)


def _expert_slabs(v, *, te):
    ne, d = v.shape
    return pl.pallas_call(
        _slab_kernel,
        grid=(ne // te,),
        in_specs=[pl.BlockSpec((te, d), lambda i: (i, 0))],
        out_specs=pl.BlockSpec((1, d, te), lambda i: (i, 0, 0)),
        out_shape=jax.ShapeDtypeStruct((ne // te, d, te), BF16),
        compiler_params=_cparams(("arbitrary",), VMEM_SMALL),
        name="expert_slabs",
    )(v)


def _row_tile_bf16(row):
    return jnp.broadcast_to(row, (BF16_SUBLANES, LANE)).astype(BF16)[None]


def _peer_kernel(ht_ref, u_ref, vt_ref, c1_ref, e1_ref, r2_ref, e2_ref, o_ref,
                 act_a, act_b, w_a, *, n_tiles):
    j = pl.program_id(0)
    te, tm = w_a.shape

    @pl.when(j == 0)
    def _():
        act_a[...] = jnp.zeros(act_a.shape, F32)

    @pl.when(jnp.maximum(j - 1, 0) % n_tiles == 0)
    def _():
        o_ref[...] = jnp.zeros(o_ref.shape, F32)

    def step(act_rd, act_wr):
        act_wr[...] = jnp.dot(u_ref[...], ht_ref[0], preferred_element_type=F32)
        for a in range(te // N_KEYS):
            rows = slice(a * N_KEYS, (a + 1) * N_KEYS)
            for c in range(tm // LANE):
                cols = slice(c * LANE, (c + 1) * LANE)
                act = act_rd[rows, cols]
                gel = 0.5 * act * (1.0 + lax.erf(act * (1.0 / math.sqrt(2.0))))
                gshape = (N_KEYS // BF16_SUBLANES, BF16_SUBLANES, LANE)
                g = jnp.zeros(gshape, BF16)
                for h in range(PEER_HEADS):
                    count = _row_tile_bf16(c1_ref[0, 0, h, a:a + 1, cols])
                    e1row = _row_tile_bf16(e1_ref[0, 0, h, a:a + 1, cols])
                    mask = jnp.clip(count - r2_ref[0, h, :, cols].reshape(gshape), 0.0, 1.0)
                    g = g + (e1row * mask) * e2_ref[0, h, :, cols].reshape(gshape)
                w_a[rows, cols] = g.reshape(N_KEYS, LANE) * gel.astype(BF16)
        o_ref[0] += jnp.dot(vt_ref[0], w_a[...], preferred_element_type=F32)

    @pl.when(j % 2 == 0)
    def _():
        step(act_a, act_b)

    @pl.when(j % 2 == 1)
    def _():
        step(act_b, act_a)


def _peer_dense(ht, u, vt, tabs):
    nb, d, tm = ht.shape
    n_tiles, _, te = vt.shape
    c1, e1, r2, e2 = tabs
    na = te // N_KEYS
    assert c1.shape == (nb, N_KEYS // na, PEER_HEADS, na, tm)
    n_pairs = nb * n_tiles
    cur = lambda g: jnp.minimum(g, n_pairs - 1)
    prv = lambda g: jnp.maximum(g - 1, 0)
    row_spec = pl.BlockSpec((1, 1, PEER_HEADS, na, tm),
                            lambda g: (prv(g) // n_tiles, prv(g) % n_tiles, 0, 0, 0))
    tab_spec = pl.BlockSpec((1, PEER_HEADS, N_KEYS, tm), lambda g: (prv(g) // n_tiles, 0, 0, 0))
    return pl.pallas_call(
        functools.partial(_peer_kernel, n_tiles=n_tiles),
        grid=(n_pairs + 1,),
        in_specs=[pl.BlockSpec((1, d, tm), lambda g: (cur(g) // n_tiles, 0, 0)),
                  pl.BlockSpec((te, d), lambda g: (cur(g) % n_tiles, 0)),
                  pl.BlockSpec((1, d, te), lambda g: (prv(g) % n_tiles, 0, 0)),
                  row_spec, row_spec, tab_spec, tab_spec],
        out_specs=pl.BlockSpec((1, d, tm), lambda g: (prv(g) // n_tiles, 0, 0)),
        out_shape=jax.ShapeDtypeStruct((nb, d, tm), F32),
        scratch_shapes=[pltpu.VMEM((te, tm), F32), pltpu.VMEM((te, tm), F32),
                        pltpu.VMEM((te, tm), BF16)],
        compiler_params=_cparams(("arbitrary",), VMEM_PEER),
        name="peer_dense",
    )(ht, u, vt, c1, e1, r2, e2)


def _final_kernel(pt_ref, x_ref, gate_ref, g_ref, o_ref):
    x2 = x_ref[...] + gate_ref[...] * pt_ref[0].T
    o_ref[...] = _rms(x2, g_ref[...])


def _final(pt, x1, gate, g, *, tf):
    nb, d, tm = pt.shape
    t = nb * tm
    per = tm // tf
    return pl.pallas_call(
        _final_kernel,
        grid=(t // tf,),
        in_specs=[pl.BlockSpec((1, d, tf), lambda i: (i // per, 0, i % per)),
                  pl.BlockSpec((tf, d), lambda i: (i, 0)),
                  pl.BlockSpec((1, d), lambda i: (0, 0)),
                  pl.BlockSpec((1, d), lambda i: (0, 0))],
        out_specs=pl.BlockSpec((tf, d), lambda i: (i, 0)),
        out_shape=jax.ShapeDtypeStruct((t, d), F32),
        compiler_params=_cparams(("arbitrary",), VMEM_SMALL),
        name="final",
    )(pt, x1, gate.reshape(1, d), g.reshape(1, d))


def _rope_perm():
    group = QK_ROPE // 2
    half = group // 2
    j = np.arange(QK_ROPE)
    return np.where((j % group) < half, j + half, j - half)


def _rope_tables(t):
    f32 = np.float32
    rows = t // GRID_W
    row = np.repeat(np.arange(rows), GRID_W).astype(f32)
    col = np.tile(np.arange(GRID_W), rows).astype(f32)
    half = QK_ROPE // 2
    freqs = (f32(ROPE_THETA) ** (-np.arange(0, half, 2, dtype=f32) / f32(half))).astype(f32)
    ar = (row[:, None] * freqs).astype(np.float64)
    ac = (col[:, None] * freqs).astype(np.float64)
    zeros = np.zeros((t, LANE - QK_ROPE))
    cos = np.concatenate([np.cos(ar), np.cos(ar), np.cos(ac), np.cos(ac), zeros], axis=1)
    sin = np.concatenate([-np.sin(ar), np.sin(ar), -np.sin(ac), np.sin(ac), zeros], axis=1)
    return jnp.asarray(cos.astype(f32)), jnp.asarray(sin.astype(f32))


def kernel(x, c, ctx, c_ctx, w_ada, b_ada, norm1_g, w_in, q_norm_g, w_uq, kv_norm_g, w_ukv,
           conv_w, conv_b, w_o, norm2_g, peer_wq, peer_keys, peer_u, peer_v, final_norm_g):
    assert x.shape == (1, SEQ, D_MODEL) and ctx.shape == (1, CTX_LEN, D_MODEL)
    assert w_ada.shape[0] == 1
    x2 = x[0]
    ctx2 = ctx[0]
    perm = _rope_perm()

    w_in0 = w_in[0]
    w_kr_perm = w_in0[:, O_KR:O_CONV][:, perm]
    w_a_x = jnp.concatenate([w_in0[:, :O_CONV], w_kr_perm], axis=1).astype(BF16)
    w_a_c = jnp.concatenate([w_in0[:, O_CKV:O_CONV], w_kr_perm], axis=1).astype(BF16)
    w_conv = w_in0[:, O_CONV:].astype(BF16)
    wq = w_uq[0].reshape(Q_LORA, MLA_HEADS, QK_NOPE + QK_ROPE)
    wq = jnp.concatenate([wq, wq[:, :, QK_NOPE:][:, :, perm]], axis=2)
    wq = wq.reshape(Q_LORA, MLA_HEADS * HEAD_PAD).astype(BF16)
    wkv = w_ukv[0].reshape(KV_LORA, MLA_HEADS, QK_NOPE + V_HEAD)
    wk = wkv[:, :, :QK_NOPE].reshape(KV_LORA, MLA_HEADS * QK_NOPE).astype(BF16)
    wvt = wkv[:, :, QK_NOPE:].reshape(KV_LORA, MLA_HEADS * V_HEAD).T.astype(BF16)
    w_pq = peer_wq[0].astype(BF16)
    keys = peer_keys[0].reshape(PEER_HEADS * 2, N_KEYS, PEER_DK // 2).astype(BF16)
    u_bf = peer_u[0].astype(BF16)
    vt_bf = _expert_slabs(peer_v[0], te=PEER_TE)

    cos_x, sin_x = _rope_tables(SEQ)
    cos_c = jnp.concatenate([jnp.ones((CTX_LEN, QK_ROPE), F32),
                             jnp.zeros((CTX_LEN, LANE - QK_ROPE), F32)], axis=1)
    sin_c = jnp.zeros((CTX_LEN, LANE), F32)

    cond8 = jnp.concatenate([c, c_ctx[None, :], jnp.zeros((F32_SUBLANES - 2, D_MODEL), F32)],
                            axis=0)
    mods = _mods(cond8, w_ada[0], b_ada[0])
    m_x = mods[0].reshape(N_MOD, D_MODEL)
    m_c = mods[1].reshape(N_MOD, D_MODEL)

    t = TILES
    pa_x, hx = _norm_proj(x2, norm1_g[0], m_x[0], m_x[1], w_a_x, tm=t["in_proj_tm"],
                          tn=w_a_x.shape[1], out_dtype=F32, cos=cos_x, sin=sin_x)
    pa_c, _ = _norm_proj(ctx2, norm1_g[0], m_c[0], m_c[1], w_a_c, tm=CTX_LEN, tn=w_a_c.shape[1],
                         out_dtype=F32, cos=cos_c, sin=sin_c)
    pconv = _matmul(hx, w_conv, tm=t["conv_proj_tm"], tn=t["conv_proj_tn"], out_dtype=F32)
    conv = _short_conv(pconv, conv_w[0], conv_b[0])

    q = _q_proj(pa_x, q_norm_g[0], wq, cos_x, sin_x, tm=t["q_proj_tm"])
    ckv = jnp.concatenate([pa_x[:, O_CKV:O_KR], pa_c[:, :KV_LORA]], axis=0)
    krot = jnp.concatenate([pa_x[:, O_KR:], pa_c[:, KV_LORA:]], axis=0)
    k, vt = _kv_proj(ckv, kv_norm_g[0], krot, wk, wvt, tm=t["kv_proj_tm"])
    att = _attention(q, k, vt, tq=t["attn_tq"])

    x1 = _out_proj(att, conv, w_o[0], x2, m_x[2], tm=t["out_proj_tm"], tn=t["out_proj_tn"])

    qp, h2t = _norm_proj(x1, norm2_g[0], m_x[3], m_x[4], w_pq, tm=t["peer_tm"],
                         tn=t["peer_q_tn"], out_dtype=BF16, transpose_h=True)
    tabs = _peer_gates(qp, keys, tt=t["peer_tm"], na=PEER_TE // N_KEYS)
    peer_t = _peer_dense(h2t, u_bf, vt_bf, tabs)
    out = _final(peer_t, x1, m_x[5], final_norm_g, tf=t["final_tf"])
    return out[None]
```

```python
import functools
import math

import numpy as np
import jax
import jax.numpy as jnp
from jax import lax
from jax.experimental import pallas as pl
from jax.experimental.pallas import tpu as pltpu

F32 = jnp.float32
BF16 = jnp.bfloat16

D_MODEL = 4096
SEQ = 8192
GRID_W = 64
CTX_LEN = 256
MLA_HEADS = 16
QK_NOPE = 128
QK_ROPE = 64
V_HEAD = 128
Q_LORA = 768
KV_LORA = 512
ROPE_THETA = 10000.0
SOFTMAX_SCALE = 1.0 / math.sqrt(QK_NOPE + QK_ROPE)
CONV_CH = 2048
PEER_HEADS = 8
N_KEYS = 128
N_EXPERTS = N_KEYS * N_KEYS
PEER_TOPK = 16
PEER_DK = 256
N_MOD = 6
EPS = 1e-6
O_CKV = Q_LORA
O_KR = Q_LORA + KV_LORA
O_CONV = O_KR + QK_ROPE

LANE = 128
F32_SUBLANES = 8
BF16_SUBLANES = 16
VMEM_MIB = 64
MIB = 1024 * 1024

HEAD_PAD = 256
KV_LEN = SEQ + CTX_LEN
KV_CHUNK = 1408
Q_SCALE = SOFTMAX_SCALE * math.log2(math.e)
QK_AHEAD = 2
PEER_TE = 512
NEG = float(np.finfo(np.float32).min)

TILES = dict(
    mods_tn=512,
    in_proj_tm=256,
    conv_proj_tm=1024, conv_proj_tn=1024,
    q_proj_tm=512,
    kv_proj_tm=768,
    attn_tq=512,
    out_proj_tm=1024, out_proj_tn=512,
    peer_tm=512,
    peer_q_tn=1024,
    final_tf=256,
)
VMEM_SMALL = 48
VMEM_LARGE = 56
VMEM_PEER = VMEM_MIB - 4


def _cparams(sem, vmem_mib):
    assert vmem_mib <= VMEM_MIB
    return pltpu.CompilerParams(dimension_semantics=sem, vmem_limit_bytes=vmem_mib * MIB)


def _rms(x, g):
    ms = jnp.mean(x * x, axis=-1, keepdims=True)
    return x * lax.rsqrt(ms + EPS) * g


def _mods_kernel(cond_ref, w_ref, b_ref, o_ref):
    c = cond_ref[...]
    s = c * (1.0 / (1.0 + jnp.exp(-c)))
    o_ref[...] = jnp.dot(s.astype(BF16), w_ref[...].astype(BF16),
                         preferred_element_type=F32) + b_ref[...]


def _mods(cond8, w, b):
    rows = cond8.shape[0]
    n = w.shape[1]
    tn = TILES["mods_tn"]
    return pl.pallas_call(
        _mods_kernel,
        grid=(n // tn,),
        in_specs=[pl.BlockSpec((rows, D_MODEL), lambda j: (0, 0)),
                  pl.BlockSpec((D_MODEL, tn), lambda j: (0, j)),
                  pl.BlockSpec((1, tn), lambda j: (0, j))],
        out_specs=pl.BlockSpec((rows, tn), lambda j: (0, j)),
        out_shape=jax.ShapeDtypeStruct((rows, n), F32),
        compiler_params=_cparams(("arbitrary",), VMEM_SMALL),
        name="mods",
    )(cond8, w, b.reshape(1, n))


def _norm_proj_kernel(*refs, rope_last, transpose_h, row_chunk):
    if rope_last:
        x_ref, g_ref, sh_ref, sc_ref, w_ref, cos_ref, sin_ref, o_ref, h_ref, h_scr = refs
    else:
        x_ref, g_ref, sh_ref, sc_ref, w_ref, o_ref, h_ref, h_scr = refs
    tm = x_ref.shape[0]

    @pl.when(pl.program_id(1) == 0)
    def _():
        for r in range(tm // row_chunk):
            rows = slice(r * row_chunk, (r + 1) * row_chunk)
            y = _rms(x_ref[rows, :], g_ref[...])
            h = y * (1.0 + sc_ref[...]) + sh_ref[...]
            hb = h.astype(BF16)
            h_scr[rows, :] = hb
            if transpose_h:
                h_ref[0, :, rows] = h.T.astype(BF16)
            else:
                h_ref[rows, :] = hb

    acc = jnp.dot(h_scr[...], w_ref[...], preferred_element_type=F32)
    if rope_last:
        n = acc.shape[1]
        o_ref[:, : n - LANE] = acc[:, : n - LANE].astype(o_ref.dtype)
        xr = acc[:, n - LANE:]
        o_ref[:, n - LANE:] = (xr * cos_ref[...]
                               + pltpu.roll(xr, LANE // 2, axis=1) * sin_ref[...]).astype(o_ref.dtype)
    else:
        o_ref[...] = acc.astype(o_ref.dtype)


def _norm_proj(x, g, shift, scale, w, *, tm, tn, out_dtype, cos=None, sin=None,
               transpose_h=False, vmem_mib=VMEM_LARGE):
    m, k = x.shape
    n = w.shape[1]
    rope_last = cos is not None
    row = lambda a: a.reshape(1, k)
    in_specs = [pl.BlockSpec((tm, k), lambda i, j: (i, 0)),
                pl.BlockSpec((1, k), lambda i, j: (0, 0)),
                pl.BlockSpec((1, k), lambda i, j: (0, 0)),
                pl.BlockSpec((1, k), lambda i, j: (0, 0)),
                pl.BlockSpec((k, tn), lambda i, j: (0, j))]
    args = [x, row(g), row(shift), row(scale), w]
    if rope_last:
        assert tn == n
        in_specs += [pl.BlockSpec((tm, LANE), lambda i, j: (i, 0)),
                     pl.BlockSpec((tm, LANE), lambda i, j: (i, 0))]
        args += [cos, sin]
    if transpose_h:
        h_shape, h_spec = (m // tm, k, tm), pl.BlockSpec((1, k, tm), lambda i, j: (i, 0, 0))
    else:
        h_shape, h_spec = (m, k), pl.BlockSpec((tm, k), lambda i, j: (i, 0))
    return pl.pallas_call(
        functools.partial(_norm_proj_kernel, rope_last=rope_last, transpose_h=transpose_h,
                          row_chunk=min(tm, LANE)),
        grid=(m // tm, n // tn),
        in_specs=in_specs,
        out_specs=[pl.BlockSpec((tm, tn), lambda i, j: (i, j)), h_spec],
        out_shape=[jax.ShapeDtypeStruct((m, n), out_dtype),
                   jax.ShapeDtypeStruct(h_shape, BF16)],
        scratch_shapes=[pltpu.VMEM((tm, k), BF16)],
        compiler_params=_cparams(("arbitrary", "arbitrary"), vmem_mib),
        name="norm_proj",
    )(*args)


def _matmul_kernel(a_ref, w_ref, o_ref):
    o_ref[...] = jnp.dot(a_ref[...], w_ref[...], preferred_element_type=F32).astype(o_ref.dtype)


def _matmul(a, w, *, tm, tn, out_dtype, vmem_mib=VMEM_SMALL):
    m, k = a.shape
    n = w.shape[1]
    return pl.pallas_call(
        _matmul_kernel,
        grid=(m // tm, n // tn),
        in_specs=[pl.BlockSpec((tm, k), lambda i, j: (i, 0)),
                  pl.BlockSpec((k, tn), lambda i, j: (0, j))],
        out_specs=pl.BlockSpec((tm, tn), lambda i, j: (i, j)),
        out_shape=jax.ShapeDtypeStruct((m, n), out_dtype),
        compiler_params=_cparams(("arbitrary", "arbitrary"), vmem_mib),
        name="matmul",
    )(a, w)


def _conv_kernel(xin_ref, gb_ref, gc_ref, w_ref, b_ref, o_ref):
    t = xin_ref.shape[0]
    u = gc_ref[...] * xin_ref[...]
    row = lax.broadcasted_iota(jnp.int32, u.shape, 0)
    prev = jnp.where(row == 0, 0.0, pltpu.roll(u, 1, axis=0))
    nxt = jnp.where(row == t - 1, 0.0, pltpu.roll(u, t - 1, axis=0))
    y = prev * w_ref[0:1, :] + u * w_ref[1:2, :] + nxt * w_ref[2:3, :] + b_ref[...]
    o_ref[...] = (gb_ref[...] * y).astype(o_ref.dtype)


def _short_conv(pconv, w, b):
    t = pconv.shape[0]
    nb = CONV_CH // LANE
    return pl.pallas_call(
        _conv_kernel,
        grid=(nb,),
        in_specs=[pl.BlockSpec((t, LANE), lambda j: (0, j)),
                  pl.BlockSpec((t, LANE), lambda j: (0, nb + j)),
                  pl.BlockSpec((t, LANE), lambda j: (0, 2 * nb + j)),
                  pl.BlockSpec((3, LANE), lambda j: (0, j)),
                  pl.BlockSpec((1, LANE), lambda j: (0, j))],
        out_specs=pl.BlockSpec((t, LANE), lambda j: (0, j)),
        out_shape=jax.ShapeDtypeStruct((t, CONV_CH), BF16),
        compiler_params=_cparams(("arbitrary",), VMEM_LARGE),
        name="short_conv",
    )(pconv, pconv, pconv, w, b.reshape(1, CONV_CH))


def _q_proj_kernel(cq_ref, g_ref, w_ref, cos_ref, sin_ref, q_ref):
    y = _rms(cq_ref[...], g_ref[...]).astype(BF16)
    acc = jnp.dot(y, w_ref[...], preferred_element_type=F32)
    cos = cos_ref[...]
    sin = sin_ref[...]
    for h in range(MLA_HEADS):
        lo = h * HEAD_PAD
        q_ref[h, :, :LANE] = (acc[:, lo:lo + LANE] * Q_SCALE).astype(BF16)
        xr = acc[:, lo + LANE:lo + HEAD_PAD]
        rope = xr * cos + pltpu.roll(xr, LANE // 2, axis=1) * sin
        q_ref[h, :, LANE:] = (rope * Q_SCALE).astype(BF16)


def _q_proj(pa, g, w, cos, sin, *, tm):
    m = pa.shape[0]
    n = w.shape[1]
    return pl.pallas_call(
        _q_proj_kernel,
        grid=(m // tm,),
        in_specs=[pl.BlockSpec((tm, Q_LORA), lambda i: (i, 0)),
                  pl.BlockSpec((1, Q_LORA), lambda i: (0, 0)),
                  pl.BlockSpec((Q_LORA, n), lambda i: (0, 0)),
                  pl.BlockSpec((tm, LANE), lambda i: (i, 0)),
                  pl.BlockSpec((tm, LANE), lambda i: (i, 0))],
        out_specs=pl.BlockSpec((MLA_HEADS, tm, HEAD_PAD), lambda i: (0, i, 0)),
        out_shape=jax.ShapeDtypeStruct((MLA_HEADS, m, HEAD_PAD), BF16),
        compiler_params=_cparams(("arbitrary",), VMEM_SMALL),
        name="q_proj",
    )(pa, g.reshape(1, Q_LORA), w, cos, sin)


def _kv_proj_kernel(ckv_ref, g_ref, kr_ref, wk_ref, wvt_ref, k_ref, vt_ref):
    y = _rms(ckv_ref[...], g_ref[...]).astype(BF16)
    kn = jnp.dot(y, wk_ref[...], preferred_element_type=F32)
    vt_ref[...] = lax.dot_general(wvt_ref[...], y, (((1,), (1,)), ((), ())),
                                  preferred_element_type=F32).astype(BF16)
    kr = kr_ref[...].astype(BF16)
    for h in range(MLA_HEADS):
        k_ref[h, :, :LANE] = kn[:, h * QK_NOPE:(h + 1) * QK_NOPE].astype(BF16)
        k_ref[h, :, LANE:] = kr


def _kv_proj(ckv, g, krot, wk, wvt, *, tm):
    m = ckv.shape[0]
    nv = MLA_HEADS * V_HEAD
    return pl.pallas_call(
        _kv_proj_kernel,
        grid=(m // tm,),
        in_specs=[pl.BlockSpec((tm, KV_LORA), lambda i: (i, 0)),
                  pl.BlockSpec((1, KV_LORA), lambda i: (0, 0)),
                  pl.BlockSpec((tm, LANE), lambda i: (i, 0)),
                  pl.BlockSpec(wk.shape, lambda i: (0, 0)),
                  pl.BlockSpec(wvt.shape, lambda i: (0, 0))],
        out_specs=[pl.BlockSpec((MLA_HEADS, tm, HEAD_PAD), lambda i: (0, i, 0)),
                   pl.BlockSpec((nv, tm), lambda i: (0, i))],
        out_shape=[jax.ShapeDtypeStruct((MLA_HEADS, m, HEAD_PAD), BF16),
                   jax.ShapeDtypeStruct((nv, m), BF16)],
        compiler_params=_cparams(("arbitrary",), VMEM_SMALL),
        name="kv_proj",
    )(ckv, g.reshape(1, KV_LORA), krot, wk, wvt)


def _attn_kernel(q_ref, k_ref, vt_ref, o_ref):
    chunk = KV_CHUNK
    n_chunks = vt_ref.shape[1] // chunk
    q = q_ref[0]
    tq = q.shape[0]
    nt = (((1,), (1,)), ((), ()))
    m = jnp.full((1, tq), NEG, F32)
    l = jnp.zeros((1, tq), F32)
    acc = jnp.zeros((V_HEAD, tq), F32)

    def scores(c):
        return lax.dot_general(k_ref[0, c * chunk:(c + 1) * chunk, :], q, nt,
                               preferred_element_type=F32)

    pending = [scores(c) for c in range(min(QK_AHEAD, n_chunks))]
    for c in range(n_chunks):
        s = pending.pop(0)
        if c + QK_AHEAD < n_chunks:
            pending.append(scores(c + QK_AHEAD))
        m_new = jnp.maximum(m, s.max(axis=0, keepdims=True))
        p = jnp.exp2(s - m_new)
        alpha = jnp.exp2(m - m_new)
        l = alpha * l + p.sum(axis=0, keepdims=True)
        acc = alpha * acc + jnp.dot(vt_ref[:, c * chunk:(c + 1) * chunk], p.astype(BF16),
                                    preferred_element_type=F32)
        m = m_new
    o_ref[...] = (acc / l).T.astype(o_ref.dtype)


def _attention(q, k, vt, *, tq):
    s = q.shape[1]
    kv = k.shape[1]
    assert kv % KV_CHUNK == 0
    return pl.pallas_call(
        _attn_kernel,
        grid=(MLA_HEADS, s // tq),
        in_specs=[pl.BlockSpec((1, tq, HEAD_PAD), lambda h, i: (h, i, 0)),
                  pl.BlockSpec((1, kv, HEAD_PAD), lambda h, i: (h, 0, 0)),
                  pl.BlockSpec((V_HEAD, kv), lambda h, i: (h, 0))],
        out_specs=pl.BlockSpec((tq, V_HEAD), lambda h, i: (i, h)),
        out_shape=jax.ShapeDtypeStruct((s, MLA_HEADS * V_HEAD), BF16),
        compiler_params=_cparams(("arbitrary", "arbitrary"), VMEM_SMALL),
        name="attention",
    )(q, k, vt)


def _out_proj_kernel(att_ref, conv_ref, wa_ref, wc_ref, x_ref, gate_ref, o_ref):
    acc = jnp.dot(att_ref[...], wa_ref[...].astype(BF16), preferred_element_type=F32)
    acc = acc + jnp.dot(conv_ref[...], wc_ref[...].astype(BF16), preferred_element_type=F32)
    o_ref[...] = x_ref[...] + gate_ref[...] * acc


def _out_proj(att, conv, w, x, gate, *, tm, tn):
    m, ka = att.shape
    kc = conv.shape[1]
    assert ka == kc and w.shape[0] == ka + kc
    n = w.shape[1]
    return pl.pallas_call(
        _out_proj_kernel,
        grid=(m // tm, n // tn),
        in_specs=[pl.BlockSpec((tm, ka), lambda i, j: (i, 0)),
                  pl.BlockSpec((tm, kc), lambda i, j: (i, 0)),
                  pl.BlockSpec((ka, tn), lambda i, j: (0, j)),
                  pl.BlockSpec((kc, tn), lambda i, j: (1, j)),
                  pl.BlockSpec((tm, tn), lambda i, j: (i, j)),
                  pl.BlockSpec((1, tn), lambda i, j: (0, j))],
        out_specs=pl.BlockSpec((tm, tn), lambda i, j: (i, j)),
        out_shape=jax.ShapeDtypeStruct((m, n), F32),
        compiler_params=_cparams(("arbitrary", "arbitrary"), VMEM_LARGE),
        name="out_proj",
    )(att, conv, w, w, x, gate.reshape(1, n))


def _top16(pieces, lanes, want_rank=False):
    rid = lax.broadcasted_iota(jnp.int32, (PEER_TOPK, lanes), 0)
    tops = jnp.zeros((PEER_TOPK, lanes), F32)
    ranks = [jnp.full(p.shape, float(PEER_TOPK), F32) for p in pieces]
    for k in range(PEER_TOPK):
        slabs = [p[r:r + F32_SUBLANES] for p in pieces
                 for r in range(0, p.shape[0], F32_SUBLANES)]
        m = functools.reduce(jnp.maximum, slabs).max(axis=0, keepdims=True)
        tops = jnp.where(rid == k, m, tops)
        if want_rank or k + 1 < PEER_TOPK:
            hit = [p == m for p in pieces]
            if want_rank:
                ranks = [jnp.where(e, float(k), r) for e, r in zip(hit, ranks)]
            pieces = [jnp.where(e, NEG, p) for e, p in zip(hit, pieces)]
    return (tops, ranks) if want_rank else tops


def _peer_gate_kernel(q_ref, keys_ref, c1_ref, e1_ref, r2_ref, e2_ref):
    lanes = q_ref.shape[0]
    n_groups, na = c1_ref.shape[1], c1_ref.shape[3]
    nt = (((1,), (1,)), ((), ()))
    for h in range(PEER_HEADS):
        q1 = q_ref[:, (2 * h) * LANE:(2 * h + 1) * LANE]
        q2 = q_ref[:, (2 * h + 1) * LANE:(2 * h + 2) * LANE]
        s1_all = lax.dot_general(keys_ref[2 * h], q1, nt, preferred_element_type=F32)
        s2_all = lax.dot_general(keys_ref[2 * h + 1], q2, nt, preferred_element_type=F32)
        for c in range(lanes // LANE):
            cols = slice(c * LANE, (c + 1) * LANE)
            s1 = s1_all[:, cols]
            s2 = s2_all[:, cols]
            t1 = _top16([s1], LANE)
            t2, (rank2,) = _top16([s2], LANE, want_rank=True)
            cands = [t1[0:1] + t2]
            cands += [t1[i:i + 1] + t2[0:8] for i in range(1, 8)]
            cands += [t1[8:16] + t2[0:1]]
            top = _top16(cands, LANE)
            thr = top[PEER_TOPK - 1:PEER_TOPK]
            z = jnp.sum(jnp.exp(top - top[0:1]), axis=0, keepdims=True)
            count1 = jnp.zeros(s1.shape, F32)
            for i in range(PEER_TOPK):
                n_pass = jnp.sum(jnp.where(t1[i:i + 1] + t2 >= thr, 1.0, 0.0), axis=0,
                                 keepdims=True)
                count1 = jnp.where(s1 == t1[i:i + 1], n_pass, count1)
            e1 = jnp.exp(s1 - t1[0:1]) / z
            for r in range(n_groups):
                c1_ref[0, r, h, :, cols] = count1[r * na:(r + 1) * na, :]
                e1_ref[0, r, h, :, cols] = e1[r * na:(r + 1) * na, :]
            r2_ref[0, h, :, cols] = rank2.astype(BF16)
            e2_ref[0, h, :, cols] = jnp.exp(s2 - t2[0:1]).astype(BF16)


def _peer_gates(q, keys, *, tt, na):
    t = q.shape[0]
    nb = t // tt
    row_shape = (nb, N_KEYS // na, PEER_HEADS, na, tt)
    tab_shape = (nb, PEER_HEADS, N_KEYS, tt)
    blk = lambda shape: pl.BlockSpec((1,) + shape[1:], lambda i: (i,) + (0,) * (len(shape) - 1))
    return pl.pallas_call(
        _peer_gate_kernel,
        grid=(nb,),
        in_specs=[pl.BlockSpec((tt, PEER_HEADS * PEER_DK), lambda i: (i, 0)),
                  pl.BlockSpec(keys.shape, lambda i: (0, 0, 0))],
        out_specs=[blk(row_shape), blk(row_shape), blk(tab_shape), blk(tab_shape)],
        out_shape=[jax.ShapeDtypeStruct(row_shape, F32), jax.ShapeDtypeStruct(row_shape, F32),
                   jax.ShapeDtypeStruct(tab_shape, BF16), jax.ShapeDtypeStruct(tab_shape, BF16)],
        compiler_params=_cparams(("arbitrary",), VMEM_SMALL),
        name="peer_gates",
    )(q, keys)


def _slab_kernel(v_ref, o_ref):
    o_ref[0] = v_ref[...].T.astype(BF16)


def _expert_slabs(v, *, te):
    ne, d = v.shape
    return pl.pallas_call(
        _slab_kernel,
        grid=(ne // te,),
        in_specs=[pl.BlockSpec((te, d), lambda i: (i, 0))],
        out_specs=pl.BlockSpec((1, d, te), lambda i: (i, 0, 0)),
        out_shape=jax.ShapeDtypeStruct((ne // te, d, te), BF16),
        compiler_params=_cparams(("arbitrary",), VMEM_SMALL),
        name="expert_slabs",
    )(v)


def _row_tile_bf16(row):
    return jnp.broadcast_to(row, (BF16_SUBLANES, LANE)).astype(BF16)[None]


def _peer_kernel(ht_ref, u_ref, vt_ref, c1_ref, e1_ref, r2_ref, e2_ref, o_ref,
                 act_a, act_b, w_a, *, n_tiles):
    j = pl.program_id(0)
    te, tm = w_a.shape

    @pl.when(j == 0)
    def _():
        act_a[...] = jnp.zeros(act_a.shape, F32)

    @pl.when(jnp.maximum(j - 1, 0) % n_tiles == 0)
    def _():
        o_ref[...] = jnp.zeros(o_ref.shape, F32)

    def step(act_rd, act_wr):
        act_wr[...] = jnp.dot(u_ref[...], ht_ref[0], preferred_element_type=F32)
        for a in range(te // N_KEYS):
            rows = slice(a * N_KEYS, (a + 1) * N_KEYS)
            for c in range(tm // LANE):
                cols = slice(c * LANE, (c + 1) * LANE)
                act = act_rd[rows, cols]
                gel = 0.5 * act * (1.0 + lax.erf(act * (1.0 / math.sqrt(2.0))))
                gshape = (N_KEYS // BF16_SUBLANES, BF16_SUBLANES, LANE)
                g = jnp.zeros(gshape, BF16)
                for h in range(PEER_HEADS):
                    count = _row_tile_bf16(c1_ref[0, 0, h, a:a + 1, cols])
                    e1row = _row_tile_bf16(e1_ref[0, 0, h, a:a + 1, cols])
                    mask = jnp.clip(count - r2_ref[0, h, :, cols].reshape(gshape), 0.0, 1.0)
                    g = g + (e1row * mask) * e2_ref[0, h, :, cols].reshape(gshape)
                w_a[rows, cols] = g.reshape(N_KEYS, LANE) * gel.astype(BF16)
        o_ref[0] += jnp.dot(vt_ref[0], w_a[...], preferred_element_type=F32)

    @pl.when(j % 2 == 0)
    def _():
        step(act_a, act_b)

    @pl.when(j % 2 == 1)
    def _():
        step(act_b, act_a)


def _peer_dense(ht, u, vt, tabs):
    nb, d, tm = ht.shape
    n_tiles, _, te = vt.shape
    c1, e1, r2, e2 = tabs
    na = te // N_KEYS
    assert c1.shape == (nb, N_KEYS // na, PEER_HEADS, na, tm)
    n_pairs = nb * n_tiles
    cur = lambda g: jnp.minimum(g, n_pairs - 1)
    prv = lambda g: jnp.maximum(g - 1, 0)
    row_spec = pl.BlockSpec((1, 1, PEER_HEADS, na, tm),
                            lambda g: (prv(g) // n_tiles, prv(g) % n_tiles, 0, 0, 0))
    tab_spec = pl.BlockSpec((1, PEER_HEADS, N_KEYS, tm), lambda g: (prv(g) // n_tiles, 0, 0, 0))
    return pl.pallas_call(
        functools.partial(_peer_kernel, n_tiles=n_tiles),
        grid=(n_pairs + 1,),
        in_specs=[pl.BlockSpec((1, d, tm), lambda g: (cur(g) // n_tiles, 0, 0)),
                  pl.BlockSpec((te, d), lambda g: (cur(g) % n_tiles, 0)),
                  pl.BlockSpec((1, d, te), lambda g: (prv(g) % n_tiles, 0, 0)),
                  row_spec, row_spec, tab_spec, tab_spec],
        out_specs=pl.BlockSpec((1, d, tm), lambda g: (prv(g) // n_tiles, 0, 0)),
        out_shape=jax.ShapeDtypeStruct((nb, d, tm), F32),
        scratch_shapes=[pltpu.VMEM((te, tm), F32), pltpu.VMEM((te, tm), F32),
                        pltpu.VMEM((te, tm), BF16)],
        compiler_params=_cparams(("arbitrary",), VMEM_PEER),
        name="peer_dense",
    )(ht, u, vt, c1, e1, r2, e2)


def _final_kernel(pt_ref, x_ref, gate_ref, g_ref, o_ref):
    x2 = x_ref[...] + gate_ref[...] * pt_ref[0].T
    o_ref[...] = _rms(x2, g_ref[...])


def _final(pt, x1, gate, g, *, tf):
    nb, d, tm = pt.shape
    t = nb * tm
    per = tm // tf
    return pl.pallas_call(
        _final_kernel,
        grid=(t // tf,),
        in_specs=[pl.BlockSpec((1, d, tf), lambda i: (i // per, 0, i % per)),
                  pl.BlockSpec((tf, d), lambda i: (i, 0)),
                  pl.BlockSpec((1, d), lambda i: (0, 0)),
                  pl.BlockSpec((1, d), lambda i: (0, 0))],
        out_specs=pl.BlockSpec((tf, d), lambda i: (i, 0)),
        out_shape=jax.ShapeDtypeStruct((t, d), F32),
        compiler_params=_cparams(("arbitrary",), VMEM_SMALL),
        name="final",
    )(pt, x1, gate.reshape(1, d), g.reshape(1, d))


def _rope_perm():
    group = QK_ROPE // 2
    half = group // 2
    j = np.arange(QK_ROPE)
    return np.where((j % group) < half, j + half, j - half)


def _rope_tables(t):
    f32 = np.float32
    rows = t // GRID_W
    row = np.repeat(np.arange(rows), GRID_W).astype(f32)
    col = np.tile(np.arange(GRID_W), rows).astype(f32)
    half = QK_ROPE // 2
    freqs = (f32(ROPE_THETA) ** (-np.arange(0, half, 2, dtype=f32) / f32(half))).astype(f32)
    ar = (row[:, None] * freqs).astype(np.float64)
    ac = (col[:, None] * freqs).astype(np.float64)
    zeros = np.zeros((t, LANE - QK_ROPE))
    cos = np.concatenate([np.cos(ar), np.cos(ar), np.cos(ac), np.cos(ac), zeros], axis=1)
    sin = np.concatenate([-np.sin(ar), np.sin(ar), -np.sin(ac), np.sin(ac), zeros], axis=1)
    return jnp.asarray(cos.astype(f32)), jnp.asarray(sin.astype(f32))


def kernel(x, c, ctx, c_ctx, w_ada, b_ada, norm1_g, w_in, q_norm_g, w_uq, kv_norm_g, w_ukv,
           conv_w, conv_b, w_o, norm2_g, peer_wq, peer_keys, peer_u, peer_v, final_norm_g):
    assert x.shape == (1, SEQ, D_MODEL) and ctx.shape == (1, CTX_LEN, D_MODEL)
    assert w_ada.shape[0] == 1
    x2 = x[0]
    ctx2 = ctx[0]
    perm = _rope_perm()

    w_in0 = w_in[0]
    w_kr_perm = w_in0[:, O_KR:O_CONV][:, perm]
    w_a_x = jnp.concatenate([w_in0[:, :O_CONV], w_kr_perm], axis=1).astype(BF16)
    w_a_c = jnp.concatenate([w_in0[:, O_CKV:O_CONV], w_kr_perm], axis=1).astype(BF16)
    w_conv = w_in0[:, O_CONV:].astype(BF16)
    wq = w_uq[0].reshape(Q_LORA, MLA_HEADS, QK_NOPE + QK_ROPE)
    wq = jnp.concatenate([wq, wq[:, :, QK_NOPE:][:, :, perm]], axis=2)
    wq = wq.reshape(Q_LORA, MLA_HEADS * HEAD_PAD).astype(BF16)
    wkv = w_ukv[0].reshape(KV_LORA, MLA_HEADS, QK_NOPE + V_HEAD)
    wk = wkv[:, :, :QK_NOPE].reshape(KV_LORA, MLA_HEADS * QK_NOPE).astype(BF16)
    wvt = wkv[:, :, QK_NOPE:].reshape(KV_LORA, MLA_HEADS * V_HEAD).T.astype(BF16)
    w_pq = peer_wq[0].astype(BF16)
    keys = peer_keys[0].reshape(PEER_HEADS * 2, N_KEYS, PEER_DK // 2).astype(BF16)
    u_bf = peer_u[0].astype(BF16)
    vt_bf = _expert_slabs(peer_v[0], te=PEER_TE)

    cos_x, sin_x = _rope_tables(SEQ)
    cos_c = jnp.concatenate([jnp.ones((CTX_LEN, QK_ROPE), F32),
                             jnp.zeros((CTX_LEN, LANE - QK_ROPE), F32)], axis=1)
    sin_c = jnp.zeros((CTX_LEN, LANE), F32)

    cond8 = jnp.concatenate([c, c_ctx[None, :], jnp.zeros((F32_SUBLANES - 2, D_MODEL), F32)],
                            axis=0)
    mods = _mods(cond8, w_ada[0], b_ada[0])
    m_x = mods[0].reshape(N_MOD, D_MODEL)
    m_c = mods[1].reshape(N_MOD, D_MODEL)

    t = TILES
    pa_x, hx = _norm_proj(x2, norm1_g[0], m_x[0], m_x[1], w_a_x, tm=t["in_proj_tm"],
                          tn=w_a_x.shape[1], out_dtype=F32, cos=cos_x, sin=sin_x)
    pa_c, _ = _norm_proj(ctx2, norm1_g[0], m_c[0], m_c[1], w_a_c, tm=CTX_LEN, tn=w_a_c.shape[1],
                         out_dtype=F32, cos=cos_c, sin=sin_c)
    pconv = _matmul(hx, w_conv, tm=t["conv_proj_tm"], tn=t["conv_proj_tn"], out_dtype=F32)
    conv = _short_conv(pconv, conv_w[0], conv_b[0])

    q = _q_proj(pa_x, q_norm_g[0], wq, cos_x, sin_x, tm=t["q_proj_tm"])
    ckv = jnp.concatenate([pa_x[:, O_CKV:O_KR], pa_c[:, :KV_LORA]], axis=0)
    krot = jnp.concatenate([pa_x[:, O_KR:], pa_c[:, KV_LORA:]], axis=0)
    k, vt = _kv_proj(ckv, kv_norm_g[0], krot, wk, wvt, tm=t["kv_proj_tm"])
    att = _attention(q, k, vt, tq=t["attn_tq"])

    x1 = _out_proj(att, conv, w_o[0], x2, m_x[2], tm=t["out_proj_tm"], tn=t["out_proj_tn"])

    qp, h2t = _norm_proj(x1, norm2_g[0], m_x[3], m_x[4], w_pq, tm=t["peer_tm"],
                         tn=t["peer_q_tn"], out_dtype=BF16, transpose_h=True)
    tabs = _peer_gates(qp, keys, tt=t["peer_tm"], na=PEER_TE // N_KEYS)
    peer_t = _peer_dense(h2t, u_bf, vt_bf, tabs)
    out = _final(peer_t, x1, m_x[5], final_norm_g, tf=t["final_tf"])
    return out[None]
```
